```python
import jax, jax.numpy as jnp
from jax import lax
import numpy as np

D_MODEL = 1024
BATCH = 4
SEQ = 4096
DEPTH = 2
DEC_BATCH = 128
DEC_SEQ = 4
PAST_LEN = 16384
PAGE_SIZE = 128

N_META = 16
N_A_LAYERS = DEPTH // 2
N_B_LAYERS = DEPTH - N_A_LAYERS
RW_HEAD = 64
RW_HEADS = D_MODEL // RW_HEAD
RW_DECAY_LORA = 64
RW_A_LORA = 64
RW_GATE_LORA = 128
GN_EPS = 64e-5
MLA_HEADS = D_MODEL // 128
QK_NOPE = 128
QK_ROPE = 64
V_HEAD = 128
KV_RANK = D_MODEL // 4
Q_RANK = 3 * D_MODEL // 8
ROPE_BASE = 10000.0
Q_BLOCK = 128
ATTN_SCALE = (QK_NOPE + QK_ROPE) ** -0.5
D_FF = 4 * D_MODEL
NORM_EPS = 1e-6

kernel_name = 'rwkv7_mla_yoco_decode_step'


def rmsnorm(x, g):
    xf = x.astype(jnp.float32)
    y = xf * lax.rsqrt(jnp.mean(xf * xf, axis=-1, keepdims=True) + NORM_EPS)
    return (y * g.astype(jnp.float32)).astype(x.dtype)


def sqrelu_mlp(x, w_up, w_down):
    return jnp.square(jax.nn.relu(x @ w_up)) @ w_down


def rope(x, pos):
    half = QK_ROPE // 2
    inv_freq = ROPE_BASE ** (-jnp.arange(half, dtype=jnp.float32) / half)
    ang = pos.astype(jnp.float32)[:, None] * inv_freq[None, :]
    ang = ang.reshape(ang.shape[0], *([1] * (x.ndim - 3)), half)
    cos, sin = jnp.cos(ang), jnp.sin(ang)
    x1 = x[..., :half].astype(jnp.float32)
    x2 = x[..., half:].astype(jnp.float32)
    return jnp.concatenate([x1 * cos - x2 * sin, x1 * sin + x2 * cos], axis=-1).astype(x.dtype)


def wkv7_scan(s0, r, decay, k, v, kk, ka):
    def step(s, inp):
        r_t, d_t, k_t, v_t, kk_t, ka_t = inp
        sa = jnp.einsum('bhvk,bhk->bhv', s, kk_t)
        s = s * d_t[:, :, None, :] - sa[..., None] * ka_t[:, :, None, :] + v_t[..., None] * k_t[:, :, None, :]
        return s, jnp.einsum('bhvk,bhk->bhv', s, r_t)
    seq = tuple(jnp.moveaxis(t.astype(jnp.float32), 1, 0) for t in (r, decay, k, v, kk, ka))
    s, o = lax.scan(step, s0.astype(jnp.float32), seq)
    return s, jnp.moveaxis(o, 0, 1)


def rwkv7_time_mix(xn, prev, s0, p, i):
    B, T, D = xn.shape
    xx = prev - xn
    mu = p['rw_mu'][i]
    xr, xw, xk, xv, xa, xg = (xn + xx * mu[m] for m in range(6))
    r = xr @ p['rw_wr'][i]
    k = xk @ p['rw_wk'][i]
    v = xv @ p['rw_wv'][i]
    w_log = -jax.nn.softplus(-(p['rw_w0'][i] + jnp.tanh(xw @ p['rw_w1'][i]) @ p['rw_w2'][i])) - 0.5
    a = jax.nn.sigmoid(p['rw_a0'][i] + (xa @ p['rw_a1'][i]) @ p['rw_a2'][i])
    g = jax.nn.sigmoid(xg @ p['rw_g1'][i]) @ p['rw_g2'][i]
    heads = lambda t: t.astype(jnp.float32).reshape(B, T, RW_HEADS, RW_HEAD)
    kk = heads(k * p['rw_kk'][i])
    kk = kk / jnp.maximum(jnp.sqrt(jnp.sum(kk * kk, axis=-1, keepdims=True)), 1e-12)
    a_h = heads(a)
    k_h = heads(k * (1.0 + (a - 1.0) * p['rw_ka'][i]))
    r_h, v_h = heads(r), heads(v)
    decay = jnp.exp(-jnp.exp(heads(w_log)))
    s, o = wkv7_scan(s0, r_h, decay, k_h, v_h, kk, kk * a_h)
    mean = jnp.mean(o, axis=-1, keepdims=True)
    var = jnp.mean(jnp.square(o - mean), axis=-1, keepdims=True)
    o = ((o - mean) * lax.rsqrt(var + GN_EPS)).reshape(B, T, D)
    o = o * p['rw_lnx_g'][i].astype(jnp.float32) + p['rw_lnx_b'][i].astype(jnp.float32)
    bonus = jnp.sum(r_h * k_h * p['rw_rk'][i].astype(jnp.float32), axis=-1, keepdims=True) * v_h
    o = (o + bonus.reshape(B, T, D)).astype(xn.dtype)
    return (o * g) @ p['rw_wo'][i], s


def mla_attend(q_lat, q_rope, q_pos, segments):
    def block(args):
        ql, qr, qp = args
        scores = []
        for lat, kr, kp in segments:
            s = jnp.einsum('bhqr,bkr->bhqk', ql, lat) + jnp.einsum('bhqp,bkp->bhqk', qr, kr)
            s = s.astype(jnp.float32) * ATTN_SCALE
            scores.append(jnp.where(kp[None, None, None, :] <= qp[None, None, :, None], s, -jnp.inf))
        probs = jax.nn.softmax(jnp.concatenate(scores, axis=-1), axis=-1)
        outs, off = [], 0
        for lat, _, _ in segments:
            n = lat.shape[1]
            outs.append(jnp.einsum('bhqk,bkr->bhqr', probs[..., off:off + n].astype(lat.dtype), lat))
            off += n
        return sum(outs[1:], outs[0])
    B, H, Q, R = q_lat.shape
    if Q <= Q_BLOCK:
        return block((q_lat, q_rope, q_pos))
    nb = -(-Q // Q_BLOCK)
    pad = nb * Q_BLOCK - Q
    ql = jnp.pad(q_lat, ((0, 0), (0, 0), (0, pad), (0, 0)))
    qr = jnp.pad(q_rope, ((0, 0), (0, 0), (0, pad), (0, 0)))
    qp = jnp.pad(q_pos, (0, pad), mode='edge')
    to_blocks = lambda t: jnp.moveaxis(t.reshape(B, H, nb, Q_BLOCK, t.shape[-1]), 2, 0)
    out = lax.map(block, (to_blocks(ql), to_blocks(qr), qp.reshape(nb, Q_BLOCK)))
    return jnp.moveaxis(out, 0, 2).reshape(B, H, nb * Q_BLOCK, R)[:, :, :Q]


def mla_layer(xn, pos, segments, p, j):
    B, T, _ = xn.shape
    cq = rmsnorm(xn @ p['w_dq'][j], p['q_norm'][j])
    q = jnp.einsum('btc,chd->bthd', cq, p['w_uq'][j])
    q_pe = rope(q[..., QK_NOPE:], pos)
    q_lat = jnp.einsum('bthn,rhn->bhtr', q[..., :QK_NOPE], p['w_uk'])
    q_rope = jnp.transpose(q_pe, (0, 2, 1, 3))
    o_lat = mla_attend(q_lat, q_rope, pos, segments)
    o = jnp.einsum('bhtr,rhv->bthv', o_lat, p['w_uv']).reshape(B, T, MLA_HEADS * V_HEAD)
    return o @ p['w_o_mla'][j]


def trunk(x, pos, shift0, wkv0, past, p):
    new_wkv, new_shift = [], []
    for i in range(N_A_LAYERS):
        xn = rmsnorm(x, p['norm_mix'][i])
        prev = jnp.concatenate([shift0[i][:, None].astype(xn.dtype), xn[:, :-1]], axis=1)
        o, s = rwkv7_time_mix(xn, prev, wkv0[i], p, i)
        new_wkv.append(s)
        new_shift.append(xn[:, -1])
        x = x + o
        x = x + sqrelu_mlp(rmsnorm(x, p['norm_ffn'][i]), p['ffn_up'][i], p['ffn_down'][i])
    kv_in = rmsnorm(x, p['kv_norm'])
    lat = rmsnorm(kv_in @ p['w_dkv'], p['lat_norm'])
    krope = rope(kv_in @ p['w_kr'], pos)
    segments = past + ((lat, krope, pos),)
    for j in range(N_B_LAYERS):
        l = N_A_LAYERS + j
        x = x + mla_layer(rmsnorm(x, p['norm_mix'][l]), pos, segments, p, j)
        x = x + sqrelu_mlp(rmsnorm(x, p['norm_ffn'][l]), p['ffn_up'][l], p['ffn_down'][l])
    y = rmsnorm(x, p['norm_final'])
    return y, jnp.stack(new_wkv), jnp.stack(new_shift), lat, krope


def setup_inputs(seed: int = 0) -> dict:
    key = jax.random.key(seed)
    f32 = jnp.float32
    keys = iter(jax.random.split(key, 64))

    def normal(shape, scale=1.0):
        return jax.random.normal(next(keys), shape, f32) * scale

    def gain(shape):
        return 1.0 + 0.02 * jax.random.normal(next(keys), shape, f32)

    nA, nB, D, H, N = N_A_LAYERS, N_B_LAYERS, D_MODEL, RW_HEADS, RW_HEAD
    n_pages = PAST_LEN // PAGE_SIZE
    n_used = DEC_BATCH * n_pages
    n_phys = n_used + n_used // 4
    page_table = jax.random.permutation(next(keys), n_phys)[:n_used].reshape(DEC_BATCH, n_pages).astype(jnp.int32)
    return {
        'x_prompt': normal((BATCH, SEQ, D)),
        'x_sample': normal((DEC_BATCH, DEC_SEQ, D)),
        'state_wkv': normal((nA, DEC_BATCH, H, N, N), 0.5),
        'state_shift': normal((nA, DEC_BATCH, D)),
        'cache_latent': normal((n_phys, PAGE_SIZE, KV_RANK)),
        'cache_krope': normal((n_phys, PAGE_SIZE, QK_ROPE)),
        'page_table': page_table,
        'meta_tokens': normal((N_META, D)),
        'rw_mu': jax.random.uniform(next(keys), (nA, 6, D), f32),
        'rw_wr': normal((nA, D, D), D ** -0.5),
        'rw_wk': normal((nA, D, D), D ** -0.5),
        'rw_wv': normal((nA, D, D), D ** -0.5),
        'rw_wo': normal((nA, D, D), D ** -0.5),
        'rw_w0': jax.random.uniform(next(keys), (nA, D), f32, minval=-6.0, maxval=1.0),
        'rw_w1': normal((nA, D, RW_DECAY_LORA), D ** -0.5),
        'rw_w2': normal((nA, RW_DECAY_LORA, D), RW_DECAY_LORA ** -0.5),
        'rw_a0': normal((nA, D), 0.1),
        'rw_a1': normal((nA, D, RW_A_LORA), D ** -0.5),
        'rw_a2': normal((nA, RW_A_LORA, D), RW_A_LORA ** -0.5),
        'rw_g1': normal((nA, D, RW_GATE_LORA), D ** -0.5),
        'rw_g2': normal((nA, RW_GATE_LORA, D), RW_GATE_LORA ** -0.5),
        'rw_kk': 0.85 + normal((nA, D), 0.05),
        'rw_ka': 1.0 + normal((nA, D), 0.05),
        'rw_rk': normal((nA, H, N), 0.1),
        'rw_lnx_g': gain((nA, D)),
        'rw_lnx_b': normal((nA, D), 0.02),
        'norm_mix': gain((DEPTH, D)),
        'norm_ffn': gain((DEPTH, D)),
        'ffn_up': normal((DEPTH, D, D_FF), D ** -0.5),
        'ffn_down': normal((DEPTH, D_FF, D), D_FF ** -0.5),
        'kv_norm': gain((D,)),
        'w_dkv': normal((D, KV_RANK), D ** -0.5),
        'lat_norm': gain((KV_RANK,)),
        'w_kr': normal((D, QK_ROPE), D ** -0.5),
        'w_uk': normal((KV_RANK, MLA_HEADS, QK_NOPE), KV_RANK ** -0.5),
        'w_uv': normal((KV_RANK, MLA_HEADS, V_HEAD), KV_RANK ** -0.5),
        'w_dq': normal((nB, D, Q_RANK), D ** -0.5),
        'q_norm': gain((nB, Q_RANK)),
        'w_uq': normal((nB, Q_RANK, MLA_HEADS, QK_NOPE + QK_ROPE), Q_RANK ** -0.5),
        'w_o_mla': normal((nB, MLA_HEADS * V_HEAD, D), (MLA_HEADS * V_HEAD) ** -0.5),
        'norm_final': gain((D,)),
    }


def reference(x_prompt, x_sample, state_wkv, state_shift, cache_latent, cache_krope, page_table,
              meta_tokens, rw_mu, rw_wr, rw_wk, rw_wv, rw_wo, rw_w0, rw_w1, rw_w2, rw_a0, rw_a1, rw_a2,
              rw_g1, rw_g2, rw_kk, rw_ka, rw_rk, rw_lnx_g, rw_lnx_b, norm_mix, norm_ffn, ffn_up, ffn_down,
              kv_norm, w_dkv, lat_norm, w_kr, w_uk, w_uv, w_dq, q_norm, w_uq, w_o_mla, norm_final):
    p = dict(rw_mu=rw_mu, rw_wr=rw_wr, rw_wk=rw_wk, rw_wv=rw_wv, rw_wo=rw_wo, rw_w0=rw_w0, rw_w1=rw_w1,
             rw_w2=rw_w2, rw_a0=rw_a0, rw_a1=rw_a1, rw_a2=rw_a2, rw_g1=rw_g1, rw_g2=rw_g2, rw_kk=rw_kk,
             rw_ka=rw_ka, rw_rk=rw_rk, rw_lnx_g=rw_lnx_g, rw_lnx_b=rw_lnx_b, norm_mix=norm_mix,
             norm_ffn=norm_ffn, ffn_up=ffn_up, ffn_down=ffn_down, kv_norm=kv_norm, w_dkv=w_dkv,
             lat_norm=lat_norm, w_kr=w_kr, w_uk=w_uk, w_uv=w_uv, w_dq=w_dq, q_norm=q_norm, w_uq=w_uq,
             w_o_mla=w_o_mla, norm_final=norm_final)

    B = x_prompt.shape[0]
    meta = jnp.broadcast_to(meta_tokens[None].astype(x_prompt.dtype), (B, N_META, D_MODEL))
    xp = jnp.concatenate([meta, x_prompt], axis=1)
    pos_p = jnp.arange(xp.shape[1], dtype=jnp.int32)
    shift0 = jnp.zeros((N_A_LAYERS, B, D_MODEL), xp.dtype)
    wkv0 = jnp.zeros((N_A_LAYERS, B, RW_HEADS, RW_HEAD, RW_HEAD), jnp.float32)
    yp, wkv_p, shift_p, lat_p, krope_p = trunk(xp, pos_p, shift0, wkv0, (), p)
    y_prompt = yp[:, N_META:]

    DB = x_sample.shape[0]
    past_len = page_table.shape[1] * PAGE_SIZE
    lat_past = cache_latent[page_table].reshape(DB, past_len, KV_RANK)
    krope_past = cache_krope[page_table].reshape(DB, past_len, QK_ROPE)
    pos_past = jnp.arange(past_len, dtype=jnp.int32)
    pos_s = PAST_LEN + jnp.arange(x_sample.shape[1], dtype=jnp.int32)
    y_sample, wkv_s, shift_s, lat_s, krope_s = trunk(
        x_sample, pos_s, state_shift, state_wkv, ((lat_past, krope_past, pos_past),), p)

    return (y_prompt, y_sample, wkv_p, shift_p, lat_p, krope_p, wkv_s, shift_s, lat_s, krope_s)
```

```python
import functools
import math

import jax
import jax.numpy as jnp
from jax import lax
from jax.experimental import pallas as pl
from jax.experimental.pallas import tpu as pltpu

F32 = jnp.float32
BF16 = jnp.bfloat16

RW_HEAD = 64
QK_NOPE = 128
QK_ROPE = 64
V_HEAD = 128
N_META = 16
PAGE_SIZE = 128
NORM_EPS = 1e-6
GN_EPS = 64e-5
ROPE_BASE = 10000.0
DECAY_SCALE = math.exp(-0.5)
MASK_VALUE = -1e30

V7X_VMEM_LIMIT_BYTES = 56 * 1024 * 1024
SCAN_CHUNK = 64
SCAN_MIN_CHUNK = 16
ROW_TILE = 256
FF_CHUNK = 1024
ATTN_TILE = 128
PAGES_PER_STEP = 16


def _mm(a, b):
    return jnp.dot(a.astype(BF16), b.astype(BF16), preferred_element_type=F32)


def _mm_nt(a, b):
    return lax.dot_general(a.astype(BF16), b.astype(BF16), (((1,), (1,)), ((), ())),
                           preferred_element_type=F32)


def _split3(x):
    h = x.astype(BF16)
    r1 = x - h.astype(F32)
    m = r1.astype(BF16)
    lo = (r1 - m.astype(F32)).astype(BF16)
    return h, m, lo


def _mm_exact_lhs(sel, x):
    h, m, lo = _split3(x)
    sel = sel.astype(BF16)
    dot = lambda t: jnp.dot(sel, t, preferred_element_type=F32)
    return dot(h) + (dot(m) + dot(lo))


def _transpose_f32(x):
    n = x.shape[1]
    eye = (lax.broadcasted_iota(jnp.int32, (n, n), 0) == lax.broadcasted_iota(jnp.int32, (n, n), 1))
    eye = eye.astype(BF16)
    h, m, lo = _split3(x)
    dot = lambda t: lax.dot_general(eye, t, (((1,), (1,)), ((), ())), preferred_element_type=F32)
    return dot(h) + (dot(m) + dot(lo))


def _rms(x, g):
    return x * lax.rsqrt(jnp.mean(x * x, axis=-1, keepdims=True) + NORM_EPS) * g


def _sigmoid(z):
    return 1.0 / (1.0 + jnp.exp(-z))


def _const_spec(shape):
    nd = len(shape)
    return pl.BlockSpec(shape, lambda *_: (0,) * nd)


def _params(sem):
    return pltpu.CompilerParams(dimension_semantics=sem, vmem_limit_bytes=V7X_VMEM_LIMIT_BYTES)


def _rwkv_pre_math(xn, prev, w, outs):
    (mu, wr, wk, wv, w0, w1, w2, a0, a1, a2, g1, g2, kkp, kap, hsum, hbc) = w
    r_ref, w_ref, k_ref, v_ref, kk_ref, a_ref, g_ref = outs
    xx = prev - xn
    mix = lambda m: (xn + xx * mu[m:m + 1, :]).astype(BF16)
    r = _mm(mix(0), wr[...])
    z = w0[...] + _mm(jnp.tanh(_mm(mix(1), w1[...])), w2[...])
    wdec = -DECAY_SCALE * _sigmoid(z)
    k = _mm(mix(2), wk[...])
    v = _mm(mix(3), wv[...])
    a = _sigmoid(a0[...] + _mm(_mm(mix(4), a1[...]), a2[...]))
    g = _mm(_sigmoid(_mm(mix(5), g1[...])), g2[...])
    kkr = k * kkp[...]
    sq = kkr * kkr
    sq_h = sq.astype(BF16)
    sq_l = (sq - sq_h.astype(F32)).astype(BF16)
    ssq = (jnp.dot(sq_h, hsum[...], preferred_element_type=F32)
           + jnp.dot(sq_l, hsum[...], preferred_element_type=F32))
    inv = jnp.minimum(lax.rsqrt(ssq), 1e12)
    inv_h = inv.astype(BF16)
    inv_l = (inv - inv_h.astype(F32)).astype(BF16)
    inv_b = (jnp.dot(inv_h, hbc[...], preferred_element_type=F32)
             + jnp.dot(inv_l, hbc[...], preferred_element_type=F32))
    r_ref[...] = r.reshape(r_ref.shape)
    w_ref[...] = wdec.reshape(w_ref.shape)
    k_ref[...] = (k * (1.0 + (a - 1.0) * kap[...])).reshape(k_ref.shape)
    v_ref[...] = v.reshape(v_ref.shape)
    kk_ref[...] = (kkr * inv_b).reshape(kk_ref.shape)
    a_ref[...] = a.reshape(a_ref.shape)
    g_ref[...] = g.reshape(g_ref.shape)


def _rwkv_pre_seq_kernel(x_ref, halo_ref, s0_ref, gmix_ref, *rest):
    w, outs, tail_ref = rest[:16], rest[16:23], rest[23]
    j = pl.program_id(1)
    x = x_ref[0]
    tm = x.shape[0]
    gm = gmix_ref[...]
    xn = _rms(x, gm)
    row = lax.broadcasted_iota(jnp.int32, (tm, 1), 0)
    prev_raw = jnp.where(row == 0, halo_ref[0][7:8, :], pltpu.roll(x, 1, 0))
    prev = jnp.where((row == 0) & (j == 0), s0_ref[0], _rms(prev_raw, gm))
    tail_ref[0] = xn[tm - 8:, :]
    _rwkv_pre_math(xn, prev, w, outs)


def _rwkv_pre_flat_kernel(seq_len, x_ref, xs_ref, s0x_ref, gmix_ref, *rest):
    w, outs, xn_ref = rest[:16], rest[16:23], rest[23]
    x = x_ref[...]
    tm = x.shape[0]
    gm = gmix_ref[...]
    xn = _rms(x, gm)
    row = lax.broadcasted_iota(jnp.int32, (tm, 1), 0)
    prev = jnp.where(row % seq_len == 0, s0x_ref[...], _rms(xs_ref[...], gm))
    xn_ref[...] = xn
    _rwkv_pre_math(xn, prev, w, outs)


def _rwkv_pre_weights(p, i):
    d = p['rw_wr'].shape[-1]
    heads = d // RW_HEAD
    head_of = jnp.arange(d, dtype=jnp.int32) // RW_HEAD
    hsum = (head_of[:, None] == jnp.arange(heads, dtype=jnp.int32)[None, :]).astype(BF16)
    row2 = lambda t: t.reshape(1, -1).astype(F32)
    return (p['rw_mu'][i].astype(F32), p['rw_wr'][i].astype(BF16), p['rw_wk'][i].astype(BF16),
            p['rw_wv'][i].astype(BF16), row2(p['rw_w0'][i]), p['rw_w1'][i].astype(BF16),
            p['rw_w2'][i].astype(BF16), row2(p['rw_a0'][i]), p['rw_a1'][i].astype(BF16),
            p['rw_a2'][i].astype(BF16), p['rw_g1'][i].astype(BF16), p['rw_g2'][i].astype(BF16),
            row2(p['rw_kk'][i]), row2(p['rw_ka'][i]), hsum, hsum.T)


def _rwkv_pre_seq(x, shift0, gmix, weights):
    b, t, d = x.shape
    tm = min(ROW_TILE, t)
    assert t % tm == 0 and tm % 8 == 0
    blk = pl.BlockSpec((1, tm, d), lambda bi, j: (bi, j, 0))
    halo = pl.BlockSpec((1, 8, d), lambda bi, j: (bi, jnp.maximum(j * (tm // 8) - 1, 0), 0))
    s0 = pl.BlockSpec((1, 1, d), lambda bi, j: (bi, 0, 0))
    tail = pl.BlockSpec((1, 8, d), lambda bi, j: (bi, 0, 0))
    out = pl.pallas_call(
        _rwkv_pre_seq_kernel,
        grid=(b, t // tm),
        in_specs=[blk, halo, s0, _const_spec((1, d))] + [_const_spec(w.shape) for w in weights],
        out_specs=[blk] * 7 + [tail],
        out_shape=[jax.ShapeDtypeStruct((b, t, d), F32)] * 7 + [jax.ShapeDtypeStruct((b, 8, d), F32)],
        compiler_params=_params(("parallel", "arbitrary")),
    )(x, x, shift0.reshape(b, 1, d), gmix.reshape(1, d), *weights)
    return out[:7], out[7][:, 7, :]


def _rwkv_pre_flat(x, shift0, gmix, weights):
    b, t, d = x.shape
    n = b * t
    tm = min(ROW_TILE, n)
    assert n % tm == 0 and tm % t == 0
    xf = x.reshape(n, d)
    xs = jnp.concatenate([jnp.zeros((1, d), F32), xf[:-1]], axis=0)
    s0x = jnp.concatenate([shift0[:, None, :], jnp.zeros((b, t - 1, d), F32)], axis=1).reshape(n, d)
    blk = pl.BlockSpec((tm, d), lambda i: (i, 0))
    out = pl.pallas_call(
        functools.partial(_rwkv_pre_flat_kernel, t),
        grid=(n // tm,),
        in_specs=[blk, blk, blk, _const_spec((1, d))] + [_const_spec(w.shape) for w in weights],
        out_specs=[blk] * 8,
        out_shape=[jax.ShapeDtypeStruct((n, d), F32)] * 8,
        compiler_params=_params(("parallel",)),
    )(xf, xs, s0x, gmix.reshape(1, d), *weights)
    seven = [o.reshape(b, t, d) for o in out[:7]]
    return seven, out[7].reshape(b, t, d)[:, t - 1, :]


def _wkv_chunk_kernel(tc, c, heads, r_ref, w_ref, k_ref, v_ref, kk_ref, a_ref, g_ref,
                      rk_ref, lng_ref, lnb_ref, s0_ref, og_ref, sout_ref, h_scr):
    step = pl.program_id(1)
    n = RW_HEAD

    @pl.when(step == 0)
    def _():
        for h in range(heads):
            h_scr[h] = _transpose_f32(s0_ref[0, h])

    def load(ref):
        x = ref[0]
        if tc < c:
            x = jnp.concatenate([x, jnp.zeros((c - tc, x.shape[1]), F32)], axis=0)
        return x

    r, w, k, v, kk, a, g = (load(t) for t in (r_ref, w_ref, k_ref, v_ref, kk_ref, a_ref, g_ref))
    ri = lax.broadcasted_iota(jnp.int32, (c, c), 0)
    ci = lax.broadcasted_iota(jnp.int32, (c, c), 1)
    tril_incl = ri >= ci
    tril_strict = ri > ci
    eye_c = (ri == ci).astype(F32)
    eye_n = (lax.broadcasted_iota(jnp.int32, (n, n), 0) == lax.broadcasted_iota(jnp.int32, (n, n), 1))

    lw = _mm_exact_lhs(tril_incl, w)
    lw_end = lw[c - 1:c, :]
    p_incl = jnp.exp(lw)
    p_excl = jnp.exp(lw - w)
    p_inv = jnp.exp(-lw)
    p_tail = jnp.exp(lw_end - lw)
    p_end = jnp.exp(lw_end)
    kb = kk * a
    kt = (kk * p_excl).astype(BF16)
    rt = r * p_incl
    rt_b = rt.astype(BF16)
    nbt = (-(kb * p_inv)).astype(BF16)
    kti = (k * p_inv).astype(BF16)
    nbh = (-(kb * p_tail)).astype(BF16)
    kh = (k * p_tail).astype(BF16)
    bonus_rk = r * k * rk_ref[...]
    vb = v.astype(BF16)
    zeros_cn = jnp.zeros((c, n), BF16)

    for h in range(heads):
        sl = slice(h * n, (h + 1) * n)
        lhs1 = jnp.concatenate([kt[:, sl], rt_b[:, sl]], axis=0)
        rhs1 = jnp.concatenate([nbt[:, sl], kti[:, sl]], axis=0)
        aa = _mm_nt(lhs1, rhs1)
        a_ub = jnp.where(tril_strict, aa[:c, :c], 0.0)
        a_uk = jnp.where(tril_strict, aa[:c, c:], 0.0)
        a_rb = jnp.where(tril_incl, aa[c:, :c], 0.0)
        a_rk = jnp.where(tril_incl, aa[c:, c:], 0.0)
        tinv = eye_c + a_ub
        pw = a_ub
        span = 2
        while span < c:
            pw = _mm(pw, pw)
            tinv = tinv + _mm(pw, tinv)
            span *= 2
        av = _mm(a_uk, vb[:, sl])
        wu = _mm(tinv, jnp.concatenate([kt[:, sl], av.astype(BF16)], axis=1))
        rhs2 = jnp.concatenate(
            [wu.astype(BF16), jnp.concatenate([zeros_cn, vb[:, sl]], axis=1)], axis=0)
        zt = _mm_nt(eye_n, jnp.concatenate([nbh[:, sl], kh[:, sl]], axis=0))
        lhs2 = jnp.concatenate(
            [jnp.concatenate([a_rb, a_rk], axis=1).astype(BF16), zt.astype(BF16)], axis=0)
        res = _mm(lhs2, rhs2)
        rp = rt[:, sl] + res[:c, :n]
        o0 = res[:c, n:]
        m_mat = jnp.where(eye_n, p_end[:, sl], 0.0) + res[c:, :n]
        n_mat = res[c:, n:]
        ro = _mm(jnp.concatenate([rp, m_mat], axis=0), h_scr[h])
        o = ro[:c] + o0
        h_scr[h] = ro[c:] + n_mat
        mean = jnp.mean(o, axis=-1, keepdims=True)
        var = jnp.mean(jnp.square(o - mean), axis=-1, keepdims=True)
        on = (o - mean) * lax.rsqrt(var + GN_EPS)
        bonus = jnp.sum(bonus_rk[:, sl], axis=-1, keepdims=True) * v[:, sl]
        out = (on * lng_ref[:, sl] + lnb_ref[:, sl] + bonus) * g[:, sl]
        og_ref[0, :, sl] = out[:tc]

    @pl.when(step == pl.num_programs(1) - 1)
    def _():
        for h in range(heads):
            sout_ref[0, h] = _transpose_f32(h_scr[h])


def _wkv_scan(seven, rk, lng, lnb, state0):
    b, t, d = seven[0].shape
    heads = d // RW_HEAD
    tc = min(SCAN_CHUNK, t)
    c = max(tc, SCAN_MIN_CHUNK)
    assert t % tc == 0
    blk = pl.BlockSpec((1, tc, d), lambda bi, j: (bi, j, 0))
    st = pl.BlockSpec((1, heads, RW_HEAD, RW_HEAD), lambda bi, j: (bi, 0, 0, 0))
    og, s_out = pl.pallas_call(
        functools.partial(_wkv_chunk_kernel, tc, c, heads),
        grid=(b, t // tc),
        in_specs=[blk] * 7 + [_const_spec((1, d))] * 3 + [st],
        out_specs=[blk, st],
        out_shape=[jax.ShapeDtypeStruct((b, t, d), F32),
                   jax.ShapeDtypeStruct((b, heads, RW_HEAD, RW_HEAD), F32)],
        scratch_shapes=[pltpu.VMEM((heads, RW_HEAD, RW_HEAD), F32)],
        compiler_params=_params(("parallel", "arbitrary")),
    )(*seven, rk.reshape(1, d).astype(F32), lng.reshape(1, d).astype(F32),
      lnb.reshape(1, d).astype(F32), state0)
    return og, s_out


def _post_kernel(mla_heads, final_norm, mix_ref, x_ref, *rest):
    if mla_heads:
        wuv_ref, rest = rest[0], rest[1:]
    wo_ref, nf_ref, up_ref, down_ref = rest[:4]
    rest = rest[4:]
    if final_norm:
        fn_ref, rest = rest[0], rest[1:]
    y_ref = rest[0]
    mix = mix_ref[...]
    if mla_heads:
        rank = mix.shape[1] // mla_heads
        mix = jnp.concatenate(
            [_mm(mix[:, h * rank:(h + 1) * rank], wuv_ref[h]) for h in range(mla_heads)], axis=1)
    x2 = x_ref[...] + _mm(mix, wo_ref[...])
    hn = _rms(x2, nf_ref[...]).astype(BF16)
    acc = x2
    dff = up_ref.shape[1]
    fc = min(FF_CHUNK, dff)
    for cidx in range(dff // fc):
        hid = jnp.dot(hn, up_ref[:, cidx * fc:(cidx + 1) * fc], preferred_element_type=F32)
        hid = jnp.square(jnp.maximum(hid, 0.0)).astype(BF16)
        acc = acc + jnp.dot(hid, down_ref[cidx * fc:(cidx + 1) * fc, :], preferred_element_type=F32)
    if final_norm:
        acc = _rms(acc, fn_ref[...])
    y_ref[...] = acc


def _post(mix, x, wo, nf, up, down, wuv=None, fn=None):
    n, d = x.shape
    tm = min(ROW_TILE, n)
    assert n % tm == 0
    ins = [mix, x]
    specs = [pl.BlockSpec((tm, mix.shape[1]), lambda i: (i, 0)), pl.BlockSpec((tm, d), lambda i: (i, 0))]

    def add_const(arr):
        ins.append(arr)
        specs.append(pl.BlockSpec(arr.shape, lambda i, _nd=arr.ndim: (0,) * _nd,
                                  pipeline_mode=pl.Buffered(1)))

    if wuv is not None:
        add_const(wuv)
    for arr in (wo, nf.reshape(1, d).astype(F32), up, down):
        add_const(arr)
    if fn is not None:
        add_const(fn.reshape(1, d).astype(F32))
    return pl.pallas_call(
        functools.partial(_post_kernel, 0 if wuv is None else wuv.shape[0], fn is not None),
        grid=(n // tm,),
        in_specs=specs,
        out_specs=pl.BlockSpec((tm, d), lambda i: (i, 0)),
        out_shape=jax.ShapeDtypeStruct((n, d), F32),
        compiler_params=_params(("parallel",)),
    )(*ins)


def _mla_pre_kernel(heads, scale, x_ref, cos_ref, sin_ref, kvn_ref, wdkv_ref, latn_ref, wkr_ref, wkrp_ref,
                    nmix_ref, wdq_ref, qn_ref, wuqn_ref, wuqr_ref, wuqrp_ref, wukt_ref,
                    lat_ref, kr_ref, latb_ref, krb_ref, ql_ref, qr_ref):
    x = x_ref[...]
    cos = cos_ref[...]
    sin = sin_ref[...]
    kv_in = _rms(x, kvn_ref[...]).astype(BF16)
    lat = _rms(jnp.dot(kv_in, wdkv_ref[...], preferred_element_type=F32), latn_ref[...])
    kr = (jnp.dot(kv_in, wkr_ref[...], preferred_element_type=F32) * cos[:, :QK_ROPE]
          + jnp.dot(kv_in, wkrp_ref[...], preferred_element_type=F32) * sin[:, :QK_ROPE])
    lat_ref[...] = lat
    kr_ref[...] = kr
    latb_ref[...] = lat.astype(BF16)
    krb_ref[...] = kr.astype(BF16)
    xn = _rms(x, nmix_ref[...]).astype(BF16)
    cq = _rms(jnp.dot(xn, wdq_ref[...], preferred_element_type=F32), qn_ref[...]).astype(BF16)
    qn = jnp.dot(cq, wuqn_ref[...], preferred_element_type=F32)
    qr = (jnp.dot(cq, wuqr_ref[...], preferred_element_type=F32) * cos
          + jnp.dot(cq, wuqrp_ref[...], preferred_element_type=F32) * sin)
    qr_ref[...] = (qr * scale).astype(BF16)
    rank = wukt_ref.shape[2]
    for h in range(heads):
        ql = _mm(qn[:, h * QK_NOPE:(h + 1) * QK_NOPE], wukt_ref[h])
        ql_ref[:, h * rank:(h + 1) * rank] = (ql * scale).astype(BF16)


def _rope_tables(pos, heads):
    half = QK_ROPE // 2
    inv_freq = ROPE_BASE ** (-jnp.arange(half, dtype=F32) / half)
    ang = pos.astype(F32)[:, None] * inv_freq[None, :]
    cos, sin = jnp.cos(ang), jnp.sin(ang)
    cos2 = jnp.concatenate([cos, cos], axis=1)
    sin2 = jnp.concatenate([-sin, sin], axis=1)
    return jnp.tile(cos2, (1, heads)), jnp.tile(sin2, (1, heads))


def _swap_halves(wcols):
    half = QK_ROPE // 2
    return jnp.concatenate([wcols[..., half:], wcols[..., :half]], axis=-1)


def _mla_weights(p, j):
    d, rank = p['w_dkv'].shape
    heads = p['w_uk'].shape[1]
    qrank = p['w_dq'].shape[2]
    wuq = p['w_uq'][j]
    wuq_n = wuq[:, :, :QK_NOPE].reshape(qrank, heads * QK_NOPE)
    wuq_r = wuq[:, :, QK_NOPE:]
    row2 = lambda t: t.reshape(1, -1).astype(F32)
    return dict(
        kvn=row2(p['kv_norm']), wdkv=p['w_dkv'].astype(BF16), latn=row2(p['lat_norm']),
        wkr=p['w_kr'].astype(BF16), wkrp=_swap_halves(p['w_kr']).astype(BF16),
        nmix=row2(p['norm_mix'][p['rw_wr'].shape[0] + j]), wdq=p['w_dq'][j].astype(BF16), qn=row2(p['q_norm'][j]),
        wuqn=wuq_n.astype(BF16), wuqr=wuq_r.reshape(qrank, heads * QK_ROPE).astype(BF16),
        wuqrp=_swap_halves(wuq_r).reshape(qrank, heads * QK_ROPE).astype(BF16),
        wukt=jnp.transpose(p['w_uk'], (1, 2, 0)).astype(BF16),
        wuv=jnp.transpose(p['w_uv'], (1, 0, 2)).astype(BF16),
        heads=heads, rank=rank)


def _mla_pre(x, cos, sin, mw):
    n, d = x.shape
    heads, rank = mw['heads'], mw['rank']
    tm = min(ROW_TILE, n)
    assert n % tm == 0 and cos.shape[0] % tm == 0
    tab_blocks = cos.shape[0] // tm
    row = lambda width: pl.BlockSpec((tm, width), lambda i: (i, 0))
    tab = pl.BlockSpec((tm, heads * QK_ROPE), lambda i: (i % tab_blocks, 0))
    names = ('kvn', 'wdkv', 'latn', 'wkr', 'wkrp', 'nmix', 'wdq', 'qn', 'wuqn', 'wuqr', 'wuqrp', 'wukt')
    consts = [mw[k] for k in names]
    scale = float((QK_NOPE + QK_ROPE) ** -0.5)
    return pl.pallas_call(
        functools.partial(_mla_pre_kernel, heads, scale),
        grid=(n // tm,),
        in_specs=[row(d), tab, tab] + [_const_spec(a.shape) for a in consts],
        out_specs=[row(rank), row(QK_ROPE), row(rank), row(QK_ROPE), row(heads * rank), row(heads * QK_ROPE)],
        out_shape=[jax.ShapeDtypeStruct((n, rank), F32), jax.ShapeDtypeStruct((n, QK_ROPE), F32),
                   jax.ShapeDtypeStruct((n, rank), BF16), jax.ShapeDtypeStruct((n, QK_ROPE), BF16),
                   jax.ShapeDtypeStruct((n, heads * rank), BF16),
                   jax.ShapeDtypeStruct((n, heads * QK_ROPE), BF16)],
        compiler_params=_params(("parallel",)),
    )(x, cos, sin, *consts)


def _softmax_update(s, vals, m_scr, l_scr, acc_scr):
    m_prev = m_scr[...]
    m_new = jnp.maximum(m_prev, jnp.max(s, axis=-1, keepdims=True))
    alpha = jnp.exp(m_prev - m_new)
    p = jnp.exp(s - m_new)
    l_scr[...] = alpha * l_scr[...] + jnp.sum(p, axis=-1, keepdims=True)
    acc_scr[...] = alpha * acc_scr[...] + jnp.dot(p.astype(BF16), vals, preferred_element_type=F32)
    m_scr[...] = m_new


def _attn_prompt_kernel(heads, ql_ref, qr_ref, lat_ref, kr_ref, plat_ref, pkr_ref, o_ref,
                        m_scr, l_scr, acc_scr):
    i = pl.program_id(1)
    tq = ql_ref.shape[0]
    rank = lat_ref.shape[2]
    ql = jnp.concatenate([ql_ref[:, h * rank:(h + 1) * rank] for h in range(heads)], axis=0)
    qr = jnp.concatenate([qr_ref[:, h * QK_ROPE:(h + 1) * QK_ROPE] for h in range(heads)], axis=0)
    m_scr[...] = jnp.full(m_scr.shape, MASK_VALUE, F32)
    l_scr[...] = jnp.zeros(l_scr.shape, F32)
    acc_scr[...] = jnp.zeros(acc_scr.shape, F32)
    plat = plat_ref[...]
    _softmax_update(_mm_nt(ql, plat) + _mm_nt(qr, pkr_ref[...]), plat, m_scr, l_scr, acc_scr)
    q_pos = i * tq + lax.broadcasted_iota(jnp.int32, (heads * tq, tq), 0) % tq
    k_off = lax.broadcasted_iota(jnp.int32, (heads * tq, tq), 1)

    def body(jb, carry):
        start = pl.multiple_of(jb * tq, tq)
        kl = lat_ref[0, pl.ds(start, tq), :]
        s = _mm_nt(ql, kl) + _mm_nt(qr, kr_ref[0, pl.ds(start, tq), :])
        s = jnp.where(k_off + jb * tq <= q_pos, s, MASK_VALUE)
        _softmax_update(s, kl, m_scr, l_scr, acc_scr)
        return carry

    lax.fori_loop(0, i + 1, body, 0)
    o = (acc_scr[...] / l_scr[...]).astype(BF16)
    for h in range(heads):
        o_ref[:, h * rank:(h + 1) * rank] = o[h * tq:(h + 1) * tq, :]


def _attn_prompt(ql, qr, latb, krb, plat, pkr, batch, heads):
    n = ql.shape[0]
    t = n // batch
    rank = latb.shape[1]
    tq = min(ATTN_TILE, t)
    assert t % tq == 0
    nq = t // tq
    npre = plat.shape[0]
    return pl.pallas_call(
        functools.partial(_attn_prompt_kernel, heads),
        grid=(batch, nq),
        in_specs=[pl.BlockSpec((tq, heads * rank), lambda b, i: (b * nq + i, 0)),
                  pl.BlockSpec((tq, heads * QK_ROPE), lambda b, i: (b * nq + i, 0)),
                  pl.BlockSpec((1, t, rank), lambda b, i: (b, 0, 0)),
                  pl.BlockSpec((1, t, QK_ROPE), lambda b, i: (b, 0, 0)),
                  _const_spec((npre, rank)), _const_spec((npre, QK_ROPE))],
        out_specs=pl.BlockSpec((tq, heads * rank), lambda b, i: (b * nq + i, 0)),
        out_shape=jax.ShapeDtypeStruct((n, heads * rank), BF16),
        scratch_shapes=[pltpu.VMEM((heads * tq, 1), F32), pltpu.VMEM((heads * tq, 1), F32),
                        pltpu.VMEM((heads * tq, rank), F32)],
        compiler_params=_params(("parallel", "arbitrary")),
    )(ql, qr, latb.reshape(batch, t, rank), krb.reshape(batch, t, QK_ROPE), plat, pkr)


def _attn_sample_kernel(npages, seq, pt_ref, ql_ref, qr_ref, nlat_ref, nkr_ref, *rest):
    lat_refs, kr_refs = rest[:npages], rest[npages:2 * npages]
    o_ref, m_scr, l_scr, acc_scr = rest[2 * npages:]
    step = pl.program_id(1)
    ql = ql_ref[0]
    qr = qr_ref[0]

    @pl.when(step == 0)
    def _():
        m_scr[...] = jnp.full(m_scr.shape, MASK_VALUE, F32)
        l_scr[...] = jnp.zeros(l_scr.shape, F32)
        acc_scr[...] = jnp.zeros(acc_scr.shape, F32)

    kl = jnp.concatenate([r[0].astype(BF16) for r in lat_refs], axis=0)
    kr = jnp.concatenate([r[0].astype(BF16) for r in kr_refs], axis=0)
    _softmax_update(_mm_nt(ql, kl) + _mm_nt(qr, kr), kl, m_scr, l_scr, acc_scr)

    @pl.when(step == pl.num_programs(1) - 1)
    def _():
        nl = nlat_ref[0]
        s = _mm_nt(ql, nl) + _mm_nt(qr, nkr_ref[0])
        q_t = lax.broadcasted_iota(jnp.int32, s.shape, 0) % seq
        k_t = lax.broadcasted_iota(jnp.int32, s.shape, 1)
        s = jnp.where(k_t <= q_t, s, MASK_VALUE)
        _softmax_update(s, nl, m_scr, l_scr, acc_scr)
        o_ref[0] = (acc_scr[...] / l_scr[...]).astype(BF16)


def _attn_sample(ql, qr, nlat, nkr, cache_latent, cache_krope, page_table):
    b, rows, rank = ql.shape
    seq = nlat.shape[1]
    total_pages = page_table.shape[1]
    npages = min(PAGES_PER_STEP, total_pages)
    assert total_pages % npages == 0

    def page_spec(width, pidx):
        return pl.BlockSpec((1, PAGE_SIZE, width), lambda bi, s, pt: (pt[bi, s * npages + pidx], 0, 0))

    per_b = lambda width, r: pl.BlockSpec((1, r, width), lambda bi, s, pt: (bi, 0, 0))
    grid_spec = pltpu.PrefetchScalarGridSpec(
        num_scalar_prefetch=1,
        grid=(b, total_pages // npages),
        in_specs=[per_b(rank, rows), per_b(QK_ROPE, rows), per_b(rank, seq), per_b(QK_ROPE, seq)]
                 + [page_spec(rank, i) for i in range(npages)]
                 + [page_spec(QK_ROPE, i) for i in range(npages)],
        out_specs=per_b(rank, rows),
        scratch_shapes=[pltpu.VMEM((rows, 1), F32), pltpu.VMEM((rows, 1), F32), pltpu.VMEM((rows, rank), F32)],
    )
    return pl.pallas_call(
        functools.partial(_attn_sample_kernel, npages, seq),
        grid_spec=grid_spec,
        out_shape=jax.ShapeDtypeStruct((b, rows, rank), BF16),
        compiler_params=_params(("parallel", "arbitrary")),
    )(page_table, ql, qr, nlat, nkr, *([cache_latent] * npages), *([cache_krope] * npages))


def _layer0(x, shift0, wkv0, p, flat):
    b, t, d = x.shape
    weights = _rwkv_pre_weights(p, 0)
    pre = _rwkv_pre_flat if flat else _rwkv_pre_seq
    seven, shift_new = pre(x, shift0, p['norm_mix'][0], weights)
    og, wkv_new = _wkv_scan(seven, p['rw_rk'][0], p['rw_lnx_g'][0], p['rw_lnx_b'][0], wkv0)
    x1 = _post(og.reshape(b * t, d), x.reshape(b * t, d), p['rw_wo'][0].astype(BF16), p['norm_ffn'][0],
               p['ffn_up'][0].astype(BF16), p['ffn_down'][0].astype(BF16))
    return x1, wkv_new, shift_new


def _layer1_post(olat, x1, p, mw):
    return _post(olat, x1, p['w_o_mla'][0].astype(BF16), p['norm_ffn'][1], p['ffn_up'][1].astype(BF16),
                 p['ffn_down'][1].astype(BF16), wuv=mw['wuv'], fn=p['norm_final'])


def kernel(x_prompt, x_sample, state_wkv, state_shift, cache_latent, cache_krope, page_table,
           meta_tokens, rw_mu, rw_wr, rw_wk, rw_wv, rw_wo, rw_w0, rw_w1, rw_w2, rw_a0, rw_a1, rw_a2,
           rw_g1, rw_g2, rw_kk, rw_ka, rw_rk, rw_lnx_g, rw_lnx_b, norm_mix, norm_ffn, ffn_up, ffn_down,
           kv_norm, w_dkv, lat_norm, w_kr, w_uk, w_uv, w_dq, q_norm, w_uq, w_o_mla, norm_final):
    p = dict(rw_mu=rw_mu, rw_wr=rw_wr, rw_wk=rw_wk, rw_wv=rw_wv, rw_wo=rw_wo, rw_w0=rw_w0, rw_w1=rw_w1,
             rw_w2=rw_w2, rw_a0=rw_a0, rw_a1=rw_a1, rw_a2=rw_a2, rw_g1=rw_g1, rw_g2=rw_g2, rw_kk=rw_kk,
             rw_ka=rw_ka, rw_rk=rw_rk, rw_lnx_g=rw_lnx_g, rw_lnx_b=rw_lnx_b, norm_mix=norm_mix,
             norm_ffn=norm_ffn, ffn_up=ffn_up, ffn_down=ffn_down, kv_norm=kv_norm, w_dkv=w_dkv,
             lat_norm=lat_norm, w_kr=w_kr, w_uk=w_uk, w_uv=w_uv, w_dq=w_dq, q_norm=q_norm, w_uq=w_uq,
             w_o_mla=w_o_mla, norm_final=norm_final)
    assert rw_wr.shape[0] == 1 and w_dq.shape[0] == 1, "one RWKV layer followed by one MLA layer"
    bsz, seq, d = x_prompt.shape
    dbsz, dseq, _ = x_sample.shape
    rw_heads = d // RW_HEAD
    mw = _mla_weights(p, 0)
    heads, rank = mw['heads'], mw['rank']
    n_meta = meta_tokens.shape[0]
    past_len = page_table.shape[1] * PAGE_SIZE

    xm1, wkv_m, shift_m = _layer0(meta_tokens[None].astype(F32), jnp.zeros((1, d), F32),
                                  jnp.zeros((1, rw_heads, RW_HEAD, RW_HEAD), F32), p, flat=False)
    cos_m, sin_m = _rope_tables(jnp.arange(n_meta, dtype=jnp.int32), heads)
    lat_m, kr_m, latb_m, krb_m, _, _ = _mla_pre(xm1, cos_m, sin_m, mw)

    xp1, wkv_p, shift_p = _layer0(x_prompt, jnp.broadcast_to(shift_m, (bsz, d)),
                                  jnp.broadcast_to(wkv_m, (bsz,) + wkv_m.shape[1:]), p, flat=False)
    cos_p, sin_p = _rope_tables(n_meta + jnp.arange(seq, dtype=jnp.int32), heads)
    lat_p, kr_p, latb_p, krb_p, ql_p, qr_p = _mla_pre(xp1, cos_p, sin_p, mw)
    olat_p = _attn_prompt(ql_p, qr_p, latb_p, krb_p, latb_m, krb_m, bsz, heads)
    y_prompt = _layer1_post(olat_p, xp1, p, mw).reshape(bsz, seq, d)
    latent_prompt = jnp.concatenate(
        [jnp.broadcast_to(lat_m[None], (bsz, n_meta, rank)), lat_p.reshape(bsz, seq, rank)], axis=1)
    krope_prompt = jnp.concatenate(
        [jnp.broadcast_to(kr_m[None], (bsz, n_meta, QK_ROPE)), kr_p.reshape(bsz, seq, QK_ROPE)], axis=1)

    xs1, wkv_s, shift_s = _layer0(x_sample, state_shift[0], state_wkv[0], p, flat=True)
    cos_s, sin_s = _rope_tables(past_len + jnp.arange(dseq, dtype=jnp.int32), heads)
    cos_s, sin_s = jnp.tile(cos_s, (dbsz, 1)), jnp.tile(sin_s, (dbsz, 1))
    lat_s, kr_s, latb_s, krb_s, ql_s, qr_s = _mla_pre(xs1, cos_s, sin_s, mw)
    head_major = lambda q, width: jnp.transpose(q.reshape(dbsz, dseq, heads, width), (0, 2, 1, 3)).reshape(
        dbsz, heads * dseq, width)
    olat_s = _attn_sample(head_major(ql_s, rank), head_major(qr_s, QK_ROPE),
                          latb_s.reshape(dbsz, dseq, rank), krb_s.reshape(dbsz, dseq, QK_ROPE),
                          cache_latent, cache_krope, page_table)
    olat_s = jnp.transpose(olat_s.reshape(dbsz, heads, dseq, rank), (0, 2, 1, 3)).reshape(dbsz * dseq, heads * rank)
    y_sample = _layer1_post(olat_s, xs1, p, mw).reshape(dbsz, dseq, d)

    return (y_prompt, y_sample, wkv_p[None], shift_p[None], latent_prompt, krope_prompt,
            wkv_s[None], shift_s[None], lat_s.reshape(dbsz, dseq, rank), kr_s.reshape(dbsz, dseq, QK_ROPE))
```

```python
import functools
import math

import jax
import jax.numpy as jnp
from jax import lax
from jax.experimental import pallas as pl
from jax.experimental.pallas import tpu as pltpu

F32 = jnp.float32
BF16 = jnp.bfloat16

RW_HEAD = 64
QK_NOPE = 128
QK_ROPE = 64
V_HEAD = 128
N_META = 16
PAGE_SIZE = 128
NORM_EPS = 1e-6
GN_EPS = 64e-5
ROPE_BASE = 10000.0
DECAY_SCALE = math.exp(-0.5)
MASK_VALUE = -1e30

V7X_VMEM_LIMIT_BYTES = 56 * 1024 * 1024
SCAN_CHUNK = 64
SCAN_MIN_CHUNK = 16
ROW_TILE = 256
FF_CHUNK = 1024
ATTN_TILE = 128
ATTN_KEY_TILE = 512
PAGES_PER_STEP = 16
SAMPLE_SEQS_PER_STEP = 2


def _mm(a, b):
    return jnp.dot(a.astype(BF16), b.astype(BF16), preferred_element_type=F32)


def _mm_nt(a, b):
    return lax.dot_general(a.astype(BF16), b.astype(BF16), (((1,), (1,)), ((), ())),
                           preferred_element_type=F32)


def _split3(x):
    h = x.astype(BF16)
    r1 = x - h.astype(F32)
    m = r1.astype(BF16)
    lo = (r1 - m.astype(F32)).astype(BF16)
    return h, m, lo


def _mm_exact_lhs(sel, x):
    h, m, lo = _split3(x)
    sel = sel.astype(BF16)
    dot = lambda t: jnp.dot(sel, t, preferred_element_type=F32)
    return dot(h) + (dot(m) + dot(lo))


def _transpose_f32(x):
    n = x.shape[1]
    eye = (lax.broadcasted_iota(jnp.int32, (n, n), 0) == lax.broadcasted_iota(jnp.int32, (n, n), 1))
    eye = eye.astype(BF16)
    h, m, lo = _split3(x)
    dot = lambda t: lax.dot_general(eye, t, (((1,), (1,)), ((), ())), preferred_element_type=F32)
    return dot(h) + (dot(m) + dot(lo))


def _rms(x, g):
    return x * lax.rsqrt(jnp.mean(x * x, axis=-1, keepdims=True) + NORM_EPS) * g


def _sigmoid(z):
    return 1.0 / (1.0 + jnp.exp(-z))


def _const_spec(shape):
    nd = len(shape)
    return pl.BlockSpec(shape, lambda *_: (0,) * nd)


def _params(sem):
    return pltpu.CompilerParams(dimension_semantics=sem, vmem_limit_bytes=V7X_VMEM_LIMIT_BYTES)


def _rwkv_pre_math(xn, prev, w, outs):
    (mu, wr, wk, wv, w0, w1, w2, a0, a1, a2, g1, g2, kkp, kap, hsum, hbc) = w
    r_ref, w_ref, k_ref, v_ref, kk_ref, a_ref, g_ref = outs
    xx = prev - xn
    mix = lambda m: (xn + xx * mu[m:m + 1, :]).astype(BF16)
    r = _mm(mix(0), wr[...])
    z = w0[...] + _mm(jnp.tanh(_mm(mix(1), w1[...])), w2[...])
    wdec = -DECAY_SCALE * _sigmoid(z)
    k = _mm(mix(2), wk[...])
    v = _mm(mix(3), wv[...])
    a = _sigmoid(a0[...] + _mm(_mm(mix(4), a1[...]), a2[...]))
    g = _mm(_sigmoid(_mm(mix(5), g1[...])), g2[...])
    kkr = k * kkp[...]
    sq = kkr * kkr
    sq_h = sq.astype(BF16)
    sq_l = (sq - sq_h.astype(F32)).astype(BF16)
    ssq = (jnp.dot(sq_h, hsum[...], preferred_element_type=F32)
           + jnp.dot(sq_l, hsum[...], preferred_element_type=F32))
    inv = jnp.minimum(lax.rsqrt(ssq), 1e12)
    inv_h = inv.astype(BF16)
    inv_l = (inv - inv_h.astype(F32)).astype(BF16)
    inv_b = (jnp.dot(inv_h, hbc[...], preferred_element_type=F32)
             + jnp.dot(inv_l, hbc[...], preferred_element_type=F32))
    r_ref[...] = r.reshape(r_ref.shape)
    w_ref[...] = wdec.reshape(w_ref.shape)
    k_ref[...] = (k * (1.0 + (a - 1.0) * kap[...])).reshape(k_ref.shape)
    v_ref[...] = v.reshape(v_ref.shape)
    kk_ref[...] = (kkr * inv_b).reshape(kk_ref.shape)
    a_ref[...] = a.reshape(a_ref.shape)
    g_ref[...] = g.reshape(g_ref.shape)


def _rwkv_pre_seq_kernel(x_ref, halo_ref, s0_ref, gmix_ref, *rest):
    w, outs, tail_ref = rest[:16], rest[16:23], rest[23]
    j = pl.program_id(1)
    x = x_ref[0]
    tm = x.shape[0]
    gm = gmix_ref[...]
    xn = _rms(x, gm)
    row = lax.broadcasted_iota(jnp.int32, (tm, 1), 0)
    prev_raw = jnp.where(row == 0, halo_ref[0][7:8, :], pltpu.roll(x, 1, 0))
    prev = jnp.where((row == 0) & (j == 0), s0_ref[0], _rms(prev_raw, gm))
    tail_ref[0] = xn[tm - 8:, :]
    _rwkv_pre_math(xn, prev, w, outs)


def _rwkv_pre_flat_kernel(seq_len, x_ref, xs_ref, s0x_ref, gmix_ref, *rest):
    w, outs, xn_ref = rest[:16], rest[16:23], rest[23]
    x = x_ref[...]
    tm = x.shape[0]
    gm = gmix_ref[...]
    xn = _rms(x, gm)
    row = lax.broadcasted_iota(jnp.int32, (tm, 1), 0)
    prev = jnp.where(row % seq_len == 0, s0x_ref[...], _rms(xs_ref[...], gm))
    xn_ref[...] = xn
    _rwkv_pre_math(xn, prev, w, outs)


def _rwkv_pre_weights(p, i):
    d = p['rw_wr'].shape[-1]
    heads = d // RW_HEAD
    head_of = jnp.arange(d, dtype=jnp.int32) // RW_HEAD
    hsum = (head_of[:, None] == jnp.arange(heads, dtype=jnp.int32)[None, :]).astype(BF16)
    row2 = lambda t: t.reshape(1, -1).astype(F32)
    return (p['rw_mu'][i].astype(F32), p['rw_wr'][i].astype(BF16), p['rw_wk'][i].astype(BF16),
            p['rw_wv'][i].astype(BF16), row2(p['rw_w0'][i]), p['rw_w1'][i].astype(BF16),
            p['rw_w2'][i].astype(BF16), row2(p['rw_a0'][i]), p['rw_a1'][i].astype(BF16),
            p['rw_a2'][i].astype(BF16), p['rw_g1'][i].astype(BF16), p['rw_g2'][i].astype(BF16),
            row2(p['rw_kk'][i]), row2(p['rw_ka'][i]), hsum, hsum.T)


def _rwkv_pre_seq(x, shift0, gmix, weights):
    b, t, d = x.shape
    tm = min(ROW_TILE, t)
    assert t % tm == 0 and tm % 8 == 0
    blk = pl.BlockSpec((1, tm, d), lambda bi, j: (bi, j, 0))
    halo = pl.BlockSpec((1, 8, d), lambda bi, j: (bi, jnp.maximum(j * (tm // 8) - 1, 0), 0))
    s0 = pl.BlockSpec((1, 1, d), lambda bi, j: (bi, 0, 0))
    tail = pl.BlockSpec((1, 8, d), lambda bi, j: (bi, 0, 0))
    out = pl.pallas_call(
        _rwkv_pre_seq_kernel,
        grid=(b, t // tm),
        in_specs=[blk, halo, s0, _const_spec((1, d))] + [_const_spec(w.shape) for w in weights],
        out_specs=[blk] * 7 + [tail],
        out_shape=[jax.ShapeDtypeStruct((b, t, d), F32)] * 7 + [jax.ShapeDtypeStruct((b, 8, d), F32)],
        compiler_params=_params(("parallel", "arbitrary")),
    )(x, x, shift0.reshape(b, 1, d), gmix.reshape(1, d), *weights)
    return out[:7], out[7][:, 7, :]


def _rwkv_pre_flat(x, shift0, gmix, weights):
    b, t, d = x.shape
    n = b * t
    tm = min(ROW_TILE, n)
    assert n % tm == 0 and tm % t == 0
    xf = x.reshape(n, d)
    xs = jnp.concatenate([jnp.zeros((1, d), F32), xf[:-1]], axis=0)
    s0x = jnp.concatenate([shift0[:, None, :], jnp.zeros((b, t - 1, d), F32)], axis=1).reshape(n, d)
    blk = pl.BlockSpec((tm, d), lambda i: (i, 0))
    out = pl.pallas_call(
        functools.partial(_rwkv_pre_flat_kernel, t),
        grid=(n // tm,),
        in_specs=[blk, blk, blk, _const_spec((1, d))] + [_const_spec(w.shape) for w in weights],
        out_specs=[blk] * 8,
        out_shape=[jax.ShapeDtypeStruct((n, d), F32)] * 8,
        compiler_params=_params(("parallel",)),
    )(xf, xs, s0x, gmix.reshape(1, d), *weights)
    seven = [o.reshape(b, t, d) for o in out[:7]]
    return seven, out[7].reshape(b, t, d)[:, t - 1, :]


def _wkv_chunk_kernel(tc, c, heads, r_ref, w_ref, k_ref, v_ref, kk_ref, a_ref, g_ref,
                      rk_ref, lng_ref, lnb_ref, s0_ref, og_ref, sout_ref, h_scr):
    step = pl.program_id(1)
    n = RW_HEAD

    @pl.when(step == 0)
    def _():
        for h in range(heads):
            h_scr[h] = _transpose_f32(s0_ref[0, h])

    def load(ref):
        x = ref[0]
        if tc < c:
            x = jnp.concatenate([x, jnp.zeros((c - tc, x.shape[1]), F32)], axis=0)
        return x

    r, w, k, v, kk, a, g = (load(t) for t in (r_ref, w_ref, k_ref, v_ref, kk_ref, a_ref, g_ref))
    ri = lax.broadcasted_iota(jnp.int32, (c, c), 0)
    ci = lax.broadcasted_iota(jnp.int32, (c, c), 1)
    tril_incl = ri >= ci
    tril_strict = ri > ci
    eye_c = (ri == ci).astype(F32)
    eye_n = (lax.broadcasted_iota(jnp.int32, (n, n), 0) == lax.broadcasted_iota(jnp.int32, (n, n), 1))

    lw = _mm_exact_lhs(tril_incl, w)
    lw_end = lw[c - 1:c, :]
    p_incl = jnp.exp(lw)
    p_excl = jnp.exp(lw - w)
    p_inv = jnp.exp(-lw)
    p_tail = jnp.exp(lw_end - lw)
    p_end = jnp.exp(lw_end)
    kb = kk * a
    kt = (kk * p_excl).astype(BF16)
    rt = r * p_incl
    rt_b = rt.astype(BF16)
    nbt = (-(kb * p_inv)).astype(BF16)
    kti = (k * p_inv).astype(BF16)
    nbh = (-(kb * p_tail)).astype(BF16)
    kh = (k * p_tail).astype(BF16)
    bonus_rk = r * k * rk_ref[...]
    vb = v.astype(BF16)
    zeros_cn = jnp.zeros((c, n), BF16)

    hs = range(heads)
    sls = [slice(h * n, (h + 1) * n) for h in hs]
    aa = [_mm_nt(jnp.concatenate([kt[:, s], rt_b[:, s]], axis=0),
                 jnp.concatenate([nbt[:, s], kti[:, s]], axis=0)) for s in sls]
    a_ub = [jnp.where(tril_strict, x[:c, :c], 0.0) for x in aa]
    a_uk = [jnp.where(tril_strict, x[:c, c:], 0.0).astype(BF16) for x in aa]
    a_r = [jnp.concatenate([jnp.where(tril_incl, x[c:, :c], 0.0), jnp.where(tril_incl, x[c:, c:], 0.0)],
                           axis=1).astype(BF16) for x in aa]
    tinv = [eye_c + x for x in a_ub]
    pw = [x.astype(BF16) for x in a_ub]
    span = 2
    while span < c:
        pw = [jnp.dot(x, x, preferred_element_type=F32).astype(BF16) for x in pw]
        tinv = [t + jnp.dot(x, t.astype(BF16), preferred_element_type=F32) for x, t in zip(pw, tinv)]
        span *= 2
    av = [jnp.dot(x, vb[:, s], preferred_element_type=F32) for x, s in zip(a_uk, sls)]
    wu = [_mm(t, jnp.concatenate([kt[:, s], x.astype(BF16)], axis=1))
          for t, x, s in zip(tinv, av, sls)]
    zt = [_mm_nt(eye_n, jnp.concatenate([nbh[:, s], kh[:, s]], axis=0)) for s in sls]
    res = [_mm(jnp.concatenate([x, z.astype(BF16)], axis=0),
               jnp.concatenate([y.astype(BF16), jnp.concatenate([zeros_cn, vb[:, s]], axis=1)], axis=0))
           for x, z, y, s in zip(a_r, zt, wu, sls)]
    ro = [_mm(jnp.concatenate([rt[:, s] + x[:c, :n],
                               jnp.where(eye_n, p_end[:, s], 0.0) + x[c:, :n]], axis=0), h_scr[h])
          for h, (x, s) in enumerate(zip(res, sls))]
    for h in hs:
        s = sls[h]
        o = ro[h][:c] + res[h][:c, n:]
        h_scr[h] = ro[h][c:] + res[h][c:, n:]
        mean = jnp.mean(o, axis=-1, keepdims=True)
        var = jnp.mean(jnp.square(o - mean), axis=-1, keepdims=True)
        on = (o - mean) * lax.rsqrt(var + GN_EPS)
        bonus = jnp.sum(bonus_rk[:, s], axis=-1, keepdims=True) * v[:, s]
        out = (on * lng_ref[:, s] + lnb_ref[:, s] + bonus) * g[:, s]
        og_ref[0, :, s] = out[:tc]

    @pl.when(step == pl.num_programs(1) - 1)
    def _():
        for h in range(heads):
            sout_ref[0, h] = _transpose_f32(h_scr[h])


def _wkv_scan(seven, rk, lng, lnb, state0):
    b, t, d = seven[0].shape
    heads = d // RW_HEAD
    tc = min(SCAN_CHUNK, t)
    c = max(tc, SCAN_MIN_CHUNK)
    assert t % tc == 0
    blk = pl.BlockSpec((1, tc, d), lambda bi, j: (bi, j, 0))
    st = pl.BlockSpec((1, heads, RW_HEAD, RW_HEAD), lambda bi, j: (bi, 0, 0, 0))
    og, s_out = pl.pallas_call(
        functools.partial(_wkv_chunk_kernel, tc, c, heads),
        grid=(b, t // tc),
        in_specs=[blk] * 7 + [_const_spec((1, d))] * 3 + [st],
        out_specs=[blk, st],
        out_shape=[jax.ShapeDtypeStruct((b, t, d), F32),
                   jax.ShapeDtypeStruct((b, heads, RW_HEAD, RW_HEAD), F32)],
        scratch_shapes=[pltpu.VMEM((heads, RW_HEAD, RW_HEAD), F32)],
        compiler_params=_params(("parallel", "arbitrary")),
    )(*seven, rk.reshape(1, d).astype(F32), lng.reshape(1, d).astype(F32),
      lnb.reshape(1, d).astype(F32), state0)
    return og, s_out


def _post_kernel(mla_heads, final_norm, mix_ref, x_ref, *rest):
    if mla_heads:
        wuv_ref, rest = rest[0], rest[1:]
    wo_ref, nf_ref, up_ref, down_ref = rest[:4]
    rest = rest[4:]
    if final_norm:
        fn_ref, rest = rest[0], rest[1:]
    y_ref = rest[0]
    mix = mix_ref[...]
    if mla_heads:
        rank = mix.shape[1] // mla_heads
        mix = jnp.concatenate(
            [_mm(mix[:, h * rank:(h + 1) * rank], wuv_ref[h]) for h in range(mla_heads)], axis=1)
    x2 = x_ref[...] + _mm(mix, wo_ref[...])
    hn = _rms(x2, nf_ref[...]).astype(BF16)
    acc = x2
    dff = up_ref.shape[1]
    fc = min(FF_CHUNK, dff)
    for cidx in range(dff // fc):
        hid = jnp.dot(hn, up_ref[:, cidx * fc:(cidx + 1) * fc], preferred_element_type=F32)
        hid = jnp.square(jnp.maximum(hid, 0.0)).astype(BF16)
        acc = acc + jnp.dot(hid, down_ref[cidx * fc:(cidx + 1) * fc, :], preferred_element_type=F32)
    if final_norm:
        acc = _rms(acc, fn_ref[...])
    y_ref[...] = acc


def _post(mix, x, wo, nf, up, down, wuv=None, fn=None):
    n, d = x.shape
    tm = min(ROW_TILE, n)
    assert n % tm == 0
    ins = [mix, x]
    specs = [pl.BlockSpec((tm, mix.shape[1]), lambda i: (i, 0)), pl.BlockSpec((tm, d), lambda i: (i, 0))]

    def add_const(arr):
        ins.append(arr)
        specs.append(pl.BlockSpec(arr.shape, lambda i, _nd=arr.ndim: (0,) * _nd,
                                  pipeline_mode=pl.Buffered(1)))

    if wuv is not None:
        add_const(wuv)
    for arr in (wo, nf.reshape(1, d).astype(F32), up, down):
        add_const(arr)
    if fn is not None:
        add_const(fn.reshape(1, d).astype(F32))
    return pl.pallas_call(
        functools.partial(_post_kernel, 0 if wuv is None else wuv.shape[0], fn is not None),
        grid=(n // tm,),
        in_specs=specs,
        out_specs=pl.BlockSpec((tm, d), lambda i: (i, 0)),
        out_shape=jax.ShapeDtypeStruct((n, d), F32),
        compiler_params=_params(("parallel",)),
    )(*ins)


def _mla_pre_kernel(heads, scale, x_ref, cos_ref, sin_ref, kvn_ref, wdkv_ref, latn_ref, wkr_ref, wkrp_ref,
                    nmix_ref, wdq_ref, qn_ref, wuqn_ref, wuqr_ref, wuqrp_ref, wukt_ref,
                    lat_ref, kr_ref, kc_ref, qc_ref):
    x = x_ref[...]
    cos = cos_ref[...]
    sin = sin_ref[...]
    rows = x.shape[0]
    rank = wukt_ref.shape[2]
    kw = kc_ref.shape[1]
    pad = kw - rank - QK_ROPE
    kv_in = _rms(x, kvn_ref[...]).astype(BF16)
    lat = _rms(jnp.dot(kv_in, wdkv_ref[...], preferred_element_type=F32), latn_ref[...])
    kr = (jnp.dot(kv_in, wkr_ref[...], preferred_element_type=F32) * cos[:, :QK_ROPE]
          + jnp.dot(kv_in, wkrp_ref[...], preferred_element_type=F32) * sin[:, :QK_ROPE])
    lat_ref[...] = lat
    kr_ref[...] = kr
    kc_ref[:, :rank] = lat.astype(BF16)
    kc_ref[:, rank:rank + QK_ROPE] = kr.astype(BF16)
    if pad:
        kc_ref[:, rank + QK_ROPE:] = jnp.zeros((rows, pad), BF16)
    xn = _rms(x, nmix_ref[...]).astype(BF16)
    cq = _rms(jnp.dot(xn, wdq_ref[...], preferred_element_type=F32), qn_ref[...]).astype(BF16)
    qn = jnp.dot(cq, wuqn_ref[...], preferred_element_type=F32)
    qr = ((jnp.dot(cq, wuqr_ref[...], preferred_element_type=F32) * cos
           + jnp.dot(cq, wuqrp_ref[...], preferred_element_type=F32) * sin) * scale).astype(BF16)
    for h in range(heads):
        ql = _mm(qn[:, h * QK_NOPE:(h + 1) * QK_NOPE], wukt_ref[h])
        qc_ref[:, h * kw:h * kw + rank] = (ql * scale).astype(BF16)
        qc_ref[:, h * kw + rank:h * kw + rank + QK_ROPE] = qr[:, h * QK_ROPE:(h + 1) * QK_ROPE]
        if pad:
            qc_ref[:, h * kw + rank + QK_ROPE:(h + 1) * kw] = jnp.zeros((rows, pad), BF16)


def _rope_tables(pos, heads):
    half = QK_ROPE // 2
    inv_freq = ROPE_BASE ** (-jnp.arange(half, dtype=F32) / half)
    ang = pos.astype(F32)[:, None] * inv_freq[None, :]
    cos, sin = jnp.cos(ang), jnp.sin(ang)
    cos2 = jnp.concatenate([cos, cos], axis=1)
    sin2 = jnp.concatenate([-sin, sin], axis=1)
    return jnp.tile(cos2, (1, heads)), jnp.tile(sin2, (1, heads))


def _swap_halves(wcols):
    half = QK_ROPE // 2
    return jnp.concatenate([wcols[..., half:], wcols[..., :half]], axis=-1)


def _mla_weights(p, j):
    d, rank = p['w_dkv'].shape
    heads = p['w_uk'].shape[1]
    qrank = p['w_dq'].shape[2]
    wuq = p['w_uq'][j]
    wuq_n = wuq[:, :, :QK_NOPE].reshape(qrank, heads * QK_NOPE)
    wuq_r = wuq[:, :, QK_NOPE:]
    row2 = lambda t: t.reshape(1, -1).astype(F32)
    return dict(
        kvn=row2(p['kv_norm']), wdkv=p['w_dkv'].astype(BF16), latn=row2(p['lat_norm']),
        wkr=p['w_kr'].astype(BF16), wkrp=_swap_halves(p['w_kr']).astype(BF16),
        nmix=row2(p['norm_mix'][p['rw_wr'].shape[0] + j]), wdq=p['w_dq'][j].astype(BF16), qn=row2(p['q_norm'][j]),
        wuqn=wuq_n.astype(BF16), wuqr=wuq_r.reshape(qrank, heads * QK_ROPE).astype(BF16),
        wuqrp=_swap_halves(wuq_r).reshape(qrank, heads * QK_ROPE).astype(BF16),
        wukt=jnp.transpose(p['w_uk'], (1, 2, 0)).astype(BF16),
        wuv=jnp.transpose(p['w_uv'], (1, 0, 2)).astype(BF16),
        heads=heads, rank=rank)


def _key_width(rank):
    return -(-(rank + QK_ROPE) // 128) * 128


def _mla_pre(x, cos, sin, mw):
    n, d = x.shape
    heads, rank = mw['heads'], mw['rank']
    kw = _key_width(rank)
    tm = min(ROW_TILE, n)
    assert n % tm == 0 and cos.shape[0] % tm == 0
    tab_blocks = cos.shape[0] // tm
    row = lambda width: pl.BlockSpec((tm, width), lambda i: (i, 0))
    tab = pl.BlockSpec((tm, heads * QK_ROPE), lambda i: (i % tab_blocks, 0))
    names = ('kvn', 'wdkv', 'latn', 'wkr', 'wkrp', 'nmix', 'wdq', 'qn', 'wuqn', 'wuqr', 'wuqrp', 'wukt')
    consts = [mw[k] for k in names]
    scale = float((QK_NOPE + QK_ROPE) ** -0.5)
    return pl.pallas_call(
        functools.partial(_mla_pre_kernel, heads, scale),
        grid=(n // tm,),
        in_specs=[row(d), tab, tab] + [_const_spec(a.shape) for a in consts],
        out_specs=[row(rank), row(QK_ROPE), row(kw), row(heads * kw)],
        out_shape=[jax.ShapeDtypeStruct((n, rank), F32), jax.ShapeDtypeStruct((n, QK_ROPE), F32),
                   jax.ShapeDtypeStruct((n, kw), BF16), jax.ShapeDtypeStruct((n, heads * kw), BF16)],
        compiler_params=_params(("parallel",)),
    )(x, cos, sin, *consts)


def _softmax_update(s, vals, m_scr, l_scr, acc_scr):
    lanes = m_scr.shape[1]
    width = s.shape[1]
    m_prev = m_scr[...]
    m_new = jnp.maximum(m_prev, jnp.max(s, axis=-1, keepdims=True))
    alpha = jnp.exp(m_prev - m_new)
    m_wide = m_new[:, :width] if width <= lanes else jnp.tile(m_new, (1, width // lanes))
    p = jnp.exp(s - m_wide)
    l_scr[...] = alpha * l_scr[...] + jnp.sum(p, axis=-1, keepdims=True)
    acc_scr[...] = (jnp.tile(alpha, (1, acc_scr.shape[1] // lanes)) * acc_scr[...]
                    + jnp.dot(p.astype(BF16), vals, preferred_element_type=F32))
    m_scr[...] = m_new


def _softmax_init(m_scr, l_scr, acc_scr):
    m_scr[...] = jnp.full(m_scr.shape, MASK_VALUE, F32)
    l_scr[...] = jnp.zeros(l_scr.shape, F32)
    acc_scr[...] = jnp.zeros(acc_scr.shape, F32)


def _softmax_result(l_scr, acc_scr):
    return acc_scr[...] / jnp.tile(l_scr[...], (1, acc_scr.shape[1] // l_scr.shape[1]))


def _attn_prompt_kernel(heads, rank, tk, qc_ref, kc_ref, pkc_ref, o_ref, m_scr, l_scr, acc_scr):
    i = pl.program_id(1)
    tq = qc_ref.shape[0]
    kw = kc_ref.shape[2]
    qc = jnp.concatenate([qc_ref[:, h * kw:(h + 1) * kw] for h in range(heads)], axis=0)
    _softmax_init(m_scr, l_scr, acc_scr)
    pkc = pkc_ref[...]
    _softmax_update(_mm_nt(qc, pkc), pkc[:, :rank], m_scr, l_scr, acc_scr)
    n_full = (i * tq) // tk

    def body(jb, carry):
        kc = kc_ref[0, pl.ds(pl.multiple_of(jb * tk, tk), tk), :]
        _softmax_update(_mm_nt(qc, kc), kc[:, :rank], m_scr, l_scr, acc_scr)
        return carry

    lax.fori_loop(0, n_full, body, 0)
    kc = kc_ref[0, pl.ds(pl.multiple_of(n_full * tk, tk), tk), :]
    q_pos = i * tq + lax.broadcasted_iota(jnp.int32, (heads * tq, tk), 0) % tq
    k_pos = n_full * tk + lax.broadcasted_iota(jnp.int32, (heads * tq, tk), 1)
    s = jnp.where(k_pos <= q_pos, _mm_nt(qc, kc), MASK_VALUE)
    _softmax_update(s, kc[:, :rank], m_scr, l_scr, acc_scr)
    o = _softmax_result(l_scr, acc_scr).astype(BF16)
    for h in range(heads):
        o_ref[:, h * rank:(h + 1) * rank] = o[h * tq:(h + 1) * tq, :]


def _attn_prompt(qc, kc, pkc, batch, heads, rank):
    n = qc.shape[0]
    t = n // batch
    kw = kc.shape[1]
    tq = min(ATTN_TILE, t)
    tk = min(ATTN_KEY_TILE, t)
    assert t % tk == 0 and tk % tq == 0
    nq = t // tq
    return pl.pallas_call(
        functools.partial(_attn_prompt_kernel, heads, rank, tk),
        grid=(batch, nq),
        in_specs=[pl.BlockSpec((tq, heads * kw), lambda b, i: (b * nq + i, 0)),
                  pl.BlockSpec((1, t, kw), lambda b, i: (b, 0, 0)),
                  _const_spec(pkc.shape)],
        out_specs=pl.BlockSpec((tq, heads * rank), lambda b, i: (b * nq + i, 0)),
        out_shape=jax.ShapeDtypeStruct((n, heads * rank), BF16),
        scratch_shapes=[pltpu.VMEM((heads * tq, 128), F32), pltpu.VMEM((heads * tq, 128), F32),
                        pltpu.VMEM((heads * tq, rank), F32)],
        compiler_params=_params(("parallel", "arbitrary")),
    )(qc, kc.reshape(batch, t, kw), pkc)


def _attn_sample_kernel(nb, npages, seq, rank, pt_ref, qc_ref, nkc_ref, *rest):
    lat_refs, krt_refs = rest[:nb * npages], rest[nb * npages:2 * nb * npages]
    o_ref, m_scr, l_scr, acc_scr = rest[2 * nb * npages:]
    step = pl.program_id(1)

    @pl.when(step == 0)
    def _():
        _softmax_init(m_scr, l_scr, acc_scr)

    for e in range(nb):
        qc = qc_ref[e]
        pages = slice(e * npages, (e + 1) * npages)
        kl = jnp.concatenate([r[0].astype(BF16) for r in lat_refs[pages]], axis=0)
        krt = jnp.concatenate([r[0].astype(BF16) for r in krt_refs[pages]], axis=1)
        s = _mm_nt(qc[:, :rank], kl) + jnp.dot(qc[:, rank:rank + QK_ROPE], krt, preferred_element_type=F32)
        _softmax_update(s, kl, m_scr.at[e], l_scr.at[e], acc_scr.at[e])

    @pl.when(step == pl.num_programs(1) - 1)
    def _():
        for e in range(nb):
            qc = qc_ref[e]
            nkc = nkc_ref[e]
            s = _mm_nt(qc, nkc)
            q_t = lax.broadcasted_iota(jnp.int32, s.shape, 0) % seq
            k_t = lax.broadcasted_iota(jnp.int32, s.shape, 1)
            s = jnp.where(k_t <= q_t, s, MASK_VALUE)
            _softmax_update(s, nkc[:, :rank], m_scr.at[e], l_scr.at[e], acc_scr.at[e])
            o_ref[e] = _softmax_result(l_scr.at[e], acc_scr.at[e]).astype(BF16)


def _attn_sample(qc, nkc, seq, rank, cache_latent, cache_krope_t, page_table):
    b, rows, kw = qc.shape
    total_pages = page_table.shape[1]
    npages = min(PAGES_PER_STEP, total_pages)
    nb = min(SAMPLE_SEQS_PER_STEP, b)
    assert total_pages % npages == 0 and b % nb == 0

    def page_spec(shape, e, pidx):
        return pl.BlockSpec((1,) + shape, lambda bi, s, pt: (pt[bi * nb + e, s * npages + pidx], 0, 0))

    per_b = lambda arr: pl.BlockSpec((nb,) + arr.shape[1:], lambda bi, s, pt: (bi, 0, 0))
    slots = [(e, i) for e in range(nb) for i in range(npages)]
    grid_spec = pltpu.PrefetchScalarGridSpec(
        num_scalar_prefetch=1,
        grid=(b // nb, total_pages // npages),
        in_specs=[per_b(qc), per_b(nkc)]
                 + [page_spec((PAGE_SIZE, rank), e, i) for e, i in slots]
                 + [page_spec((QK_ROPE, PAGE_SIZE), e, i) for e, i in slots],
        out_specs=pl.BlockSpec((nb, rows, rank), lambda bi, s, pt: (bi, 0, 0)),
        scratch_shapes=[pltpu.VMEM((nb, rows, 128), F32), pltpu.VMEM((nb, rows, 128), F32),
                        pltpu.VMEM((nb, rows, rank), F32)],
    )
    return pl.pallas_call(
        functools.partial(_attn_sample_kernel, nb, npages, seq, rank),
        grid_spec=grid_spec,
        out_shape=jax.ShapeDtypeStruct((b, rows, rank), BF16),
        compiler_params=_params(("parallel", "arbitrary")),
    )(page_table, qc, nkc, *([cache_latent] * len(slots)), *([cache_krope_t] * len(slots)))


def _layer0(x, shift0, wkv0, p, flat):
    b, t, d = x.shape
    weights = _rwkv_pre_weights(p, 0)
    pre = _rwkv_pre_flat if flat else _rwkv_pre_seq
    seven, shift_new = pre(x, shift0, p['norm_mix'][0], weights)
    og, wkv_new = _wkv_scan(seven, p['rw_rk'][0], p['rw_lnx_g'][0], p['rw_lnx_b'][0], wkv0)
    x1 = _post(og.reshape(b * t, d), x.reshape(b * t, d), p['rw_wo'][0].astype(BF16), p['norm_ffn'][0],
               p['ffn_up'][0].astype(BF16), p['ffn_down'][0].astype(BF16))
    return x1, wkv_new, shift_new


def _layer1_post(olat, x1, p, mw):
    return _post(olat, x1, p['w_o_mla'][0].astype(BF16), p['norm_ffn'][1], p['ffn_up'][1].astype(BF16),
                 p['ffn_down'][1].astype(BF16), wuv=mw['wuv'], fn=p['norm_final'])


def kernel(x_prompt, x_sample, state_wkv, state_shift, cache_latent, cache_krope, page_table,
           meta_tokens, rw_mu, rw_wr, rw_wk, rw_wv, rw_wo, rw_w0, rw_w1, rw_w2, rw_a0, rw_a1, rw_a2,
           rw_g1, rw_g2, rw_kk, rw_ka, rw_rk, rw_lnx_g, rw_lnx_b, norm_mix, norm_ffn, ffn_up, ffn_down,
           kv_norm, w_dkv, lat_norm, w_kr, w_uk, w_uv, w_dq, q_norm, w_uq, w_o_mla, norm_final):
    p = dict(rw_mu=rw_mu, rw_wr=rw_wr, rw_wk=rw_wk, rw_wv=rw_wv, rw_wo=rw_wo, rw_w0=rw_w0, rw_w1=rw_w1,
             rw_w2=rw_w2, rw_a0=rw_a0, rw_a1=rw_a1, rw_a2=rw_a2, rw_g1=rw_g1, rw_g2=rw_g2, rw_kk=rw_kk,
             rw_ka=rw_ka, rw_rk=rw_rk, rw_lnx_g=rw_lnx_g, rw_lnx_b=rw_lnx_b, norm_mix=norm_mix,
             norm_ffn=norm_ffn, ffn_up=ffn_up, ffn_down=ffn_down, kv_norm=kv_norm, w_dkv=w_dkv,
             lat_norm=lat_norm, w_kr=w_kr, w_uk=w_uk, w_uv=w_uv, w_dq=w_dq, q_norm=q_norm, w_uq=w_uq,
             w_o_mla=w_o_mla, norm_final=norm_final)
    assert rw_wr.shape[0] == 1 and w_dq.shape[0] == 1, "one RWKV layer followed by one MLA layer"
    bsz, seq, d = x_prompt.shape
    dbsz, dseq, _ = x_sample.shape
    rw_heads = d // RW_HEAD
    mw = _mla_weights(p, 0)
    heads, rank = mw['heads'], mw['rank']
    n_meta = meta_tokens.shape[0]
    past_len = page_table.shape[1] * PAGE_SIZE

    xm1, wkv_m, shift_m = _layer0(meta_tokens[None].astype(F32), jnp.zeros((1, d), F32),
                                  jnp.zeros((1, rw_heads, RW_HEAD, RW_HEAD), F32), p, flat=False)
    cos_m, sin_m = _rope_tables(jnp.arange(n_meta, dtype=jnp.int32), heads)
    lat_m, kr_m, kc_m, _ = _mla_pre(xm1, cos_m, sin_m, mw)

    xp1, wkv_p, shift_p = _layer0(x_prompt, jnp.broadcast_to(shift_m, (bsz, d)),
                                  jnp.broadcast_to(wkv_m, (bsz,) + wkv_m.shape[1:]), p, flat=False)
    cos_p, sin_p = _rope_tables(n_meta + jnp.arange(seq, dtype=jnp.int32), heads)
    lat_p, kr_p, kc_p, qc_p = _mla_pre(xp1, cos_p, sin_p, mw)
    olat_p = _attn_prompt(qc_p, kc_p, kc_m, bsz, heads, rank)
    y_prompt = _layer1_post(olat_p, xp1, p, mw).reshape(bsz, seq, d)
    latent_prompt = jnp.concatenate(
        [jnp.broadcast_to(lat_m[None], (bsz, n_meta, rank)), lat_p.reshape(bsz, seq, rank)], axis=1)
    krope_prompt = jnp.concatenate(
        [jnp.broadcast_to(kr_m[None], (bsz, n_meta, QK_ROPE)), kr_p.reshape(bsz, seq, QK_ROPE)], axis=1)

    xs1, wkv_s, shift_s = _layer0(x_sample, state_shift[0], state_wkv[0], p, flat=True)
    cos_s, sin_s = _rope_tables(past_len + jnp.arange(dseq, dtype=jnp.int32), heads)
    cos_s, sin_s = jnp.tile(cos_s, (dbsz, 1)), jnp.tile(sin_s, (dbsz, 1))
    lat_s, kr_s, kc_s, qc_s = _mla_pre(xs1, cos_s, sin_s, mw)
    kw = kc_s.shape[1]
    qc_s = jnp.transpose(qc_s.reshape(dbsz, dseq, heads, kw), (0, 2, 1, 3)).reshape(dbsz, heads * dseq, kw)
    new_rows = -(-dseq // 16) * 16
    nkc_s = jnp.pad(kc_s.reshape(dbsz, dseq, kw), ((0, 0), (0, new_rows - dseq), (0, 0)))
    olat_s = _attn_sample(qc_s, nkc_s, dseq, rank, cache_latent, jnp.swapaxes(cache_krope, 1, 2), page_table)
    olat_s = jnp.transpose(olat_s.reshape(dbsz, heads, dseq, rank), (0, 2, 1, 3)).reshape(dbsz * dseq, heads * rank)
    y_sample = _layer1_post(olat_s, xs1, p, mw).reshape(dbsz, dseq, d)

    return (y_prompt, y_sample, wkv_p[None], shift_p[None], latent_prompt, krope_prompt,
            wkv_s[None], shift_s[None], lat_s.reshape(dbsz, dseq, rank), kr_s.reshape(dbsz, dseq, QK_ROPE))
```

```python
import functools
import math

import jax
import jax.numpy as jnp
from jax import lax
from jax.experimental import pallas as pl
from jax.experimental.pallas import tpu as pltpu

F32 = jnp.float32
BF16 = jnp.bfloat16

RW_HEAD = 64
QK_NOPE = 128
QK_ROPE = 64
V_HEAD = 128
N_META = 16
PAGE_SIZE = 128
NORM_EPS = 1e-6
GN_EPS = 64e-5
ROPE_BASE = 10000.0
DECAY_SCALE = math.exp(-0.5)
MASK_VALUE = -1e30

V7X_VMEM_LIMIT_BYTES = 56 * 1024 * 1024
SCAN_CHUNK = 64
SCAN_MIN_CHUNK = 16
SCAN_SEQS_PER_STEP = 2
ROW_TILE = 256
FF_CHUNK = 1024
ATTN_TILE = 128
ATTN_KEY_TILE = 512
SAMPLE_CHUNKS = 2


def _mm(a, b):
    return jnp.dot(a.astype(BF16), b.astype(BF16), preferred_element_type=F32)


def _mm_nt(a, b):
    return lax.dot_general(a.astype(BF16), b.astype(BF16), (((1,), (1,)), ((), ())),
                           preferred_element_type=F32)


def _split3(x):
    h = x.astype(BF16)
    r1 = x - h.astype(F32)
    m = r1.astype(BF16)
    lo = (r1 - m.astype(F32)).astype(BF16)
    return h, m, lo


def _mm_exact_lhs(sel, x):
    h, m, lo = _split3(x)
    sel = sel.astype(BF16)
    dot = lambda t: jnp.dot(sel, t, preferred_element_type=F32)
    return dot(h) + (dot(m) + dot(lo))


def _transpose_f32(x):
    n = x.shape[1]
    eye = (lax.broadcasted_iota(jnp.int32, (n, n), 0) == lax.broadcasted_iota(jnp.int32, (n, n), 1))
    eye = eye.astype(BF16)
    h, m, lo = _split3(x)
    dot = lambda t: lax.dot_general(eye, t, (((1,), (1,)), ((), ())), preferred_element_type=F32)
    return dot(h) + (dot(m) + dot(lo))


def _rms(x, g):
    return x * lax.rsqrt(jnp.mean(x * x, axis=-1, keepdims=True) + NORM_EPS) * g


def _sigmoid(z):
    return 1.0 / (1.0 + jnp.exp(-z))


def _const_spec(shape):
    nd = len(shape)
    return pl.BlockSpec(shape, lambda *_: (0,) * nd)


def _params(sem):
    return pltpu.CompilerParams(dimension_semantics=sem, vmem_limit_bytes=V7X_VMEM_LIMIT_BYTES)


def _rwkv_pre_math(xn, prev, w, outs):
    (mu, wr, wk, wv, w0, w1, w2, a0, a1, a2, g1, g2, kkp, kap, hsum, hbc) = w
    r_ref, w_ref, k_ref, v_ref, kk_ref, a_ref, g_ref = outs
    xx = prev - xn
    mix = lambda m: (xn + xx * mu[m:m + 1, :]).astype(BF16)
    r = _mm(mix(0), wr[...])
    z = w0[...] + _mm(jnp.tanh(_mm(mix(1), w1[...])), w2[...])
    wdec = -DECAY_SCALE * _sigmoid(z)
    k = _mm(mix(2), wk[...])
    v = _mm(mix(3), wv[...])
    a = _sigmoid(a0[...] + _mm(_mm(mix(4), a1[...]), a2[...]))
    g = _mm(_sigmoid(_mm(mix(5), g1[...])), g2[...])
    kkr = k * kkp[...]
    sq = kkr * kkr
    sq_h = sq.astype(BF16)
    sq_l = (sq - sq_h.astype(F32)).astype(BF16)
    ssq = (jnp.dot(sq_h, hsum[...], preferred_element_type=F32)
           + jnp.dot(sq_l, hsum[...], preferred_element_type=F32))
    inv = jnp.minimum(lax.rsqrt(ssq), 1e12)
    inv_h = inv.astype(BF16)
    inv_l = (inv - inv_h.astype(F32)).astype(BF16)
    inv_b = (jnp.dot(inv_h, hbc[...], preferred_element_type=F32)
             + jnp.dot(inv_l, hbc[...], preferred_element_type=F32))
    r_ref[...] = r.reshape(r_ref.shape)
    w_ref[...] = wdec.reshape(w_ref.shape)
    k_ref[...] = (k * (1.0 + (a - 1.0) * kap[...])).reshape(k_ref.shape)
    v_ref[...] = v.reshape(v_ref.shape)
    kk_ref[...] = (kkr * inv_b).reshape(kk_ref.shape)
    a_ref[...] = a.reshape(a_ref.shape)
    g_ref[...] = g.reshape(g_ref.shape)


def _rwkv_pre_seq_kernel(x_ref, halo_ref, s0_ref, gmix_ref, *rest):
    w, outs, tail_ref = rest[:16], rest[16:23], rest[23]
    j = pl.program_id(1)
    x = x_ref[0]
    tm = x.shape[0]
    gm = gmix_ref[...]
    xn = _rms(x, gm)
    row = lax.broadcasted_iota(jnp.int32, (tm, 1), 0)
    prev_raw = jnp.where(row == 0, halo_ref[0][7:8, :], pltpu.roll(x, 1, 0))
    prev = jnp.where((row == 0) & (j == 0), s0_ref[0], _rms(prev_raw, gm))
    tail_ref[0] = xn[tm - 8:, :]
    _rwkv_pre_math(xn, prev, w, outs)


def _rwkv_pre_flat_kernel(seq_len, x_ref, xs_ref, s0x_ref, gmix_ref, *rest):
    w, outs, xn_ref = rest[:16], rest[16:23], rest[23]
    x = x_ref[...]
    tm = x.shape[0]
    gm = gmix_ref[...]
    xn = _rms(x, gm)
    row = lax.broadcasted_iota(jnp.int32, (tm, 1), 0)
    prev = jnp.where(row % seq_len == 0, s0x_ref[...], _rms(xs_ref[...], gm))
    xn_ref[...] = xn
    _rwkv_pre_math(xn, prev, w, outs)


def _rwkv_pre_weights(p, i):
    d = p['rw_wr'].shape[-1]
    heads = d // RW_HEAD
    head_of = jnp.arange(d, dtype=jnp.int32) // RW_HEAD
    hsum = (head_of[:, None] == jnp.arange(heads, dtype=jnp.int32)[None, :]).astype(BF16)
    row2 = lambda t: t.reshape(1, -1).astype(F32)
    return (p['rw_mu'][i].astype(F32), p['rw_wr'][i].astype(BF16), p['rw_wk'][i].astype(BF16),
            p['rw_wv'][i].astype(BF16), row2(p['rw_w0'][i]), p['rw_w1'][i].astype(BF16),
            p['rw_w2'][i].astype(BF16), row2(p['rw_a0'][i]), p['rw_a1'][i].astype(BF16),
            p['rw_a2'][i].astype(BF16), p['rw_g1'][i].astype(BF16), p['rw_g2'][i].astype(BF16),
            row2(p['rw_kk'][i]), row2(p['rw_ka'][i]), hsum, hsum.T)


def _rwkv_pre_seq(x, shift0, gmix, weights):
    b, t, d = x.shape
    tm = min(ROW_TILE, t)
    assert t % tm == 0 and tm % 8 == 0
    blk = pl.BlockSpec((1, tm, d), lambda bi, j: (bi, j, 0))
    halo = pl.BlockSpec((1, 8, d), lambda bi, j: (bi, jnp.maximum(j * (tm // 8) - 1, 0), 0))
    s0 = pl.BlockSpec((1, 1, d), lambda bi, j: (bi, 0, 0))
    tail = pl.BlockSpec((1, 8, d), lambda bi, j: (bi, 0, 0))
    out = pl.pallas_call(
        _rwkv_pre_seq_kernel,
        grid=(b, t // tm),
        in_specs=[blk, halo, s0, _const_spec((1, d))] + [_const_spec(w.shape) for w in weights],
        out_specs=[blk] * 7 + [tail],
        out_shape=[jax.ShapeDtypeStruct((b, t, d), F32)] * 7 + [jax.ShapeDtypeStruct((b, 8, d), F32)],
        compiler_params=_params(("parallel", "arbitrary")),
    )(x, x, shift0.reshape(b, 1, d), gmix.reshape(1, d), *weights)
    return out[:7], out[7][:, 7, :]


def _rwkv_pre_flat(x, shift0, gmix, weights):
    b, t, d = x.shape
    n = b * t
    tm = min(ROW_TILE, n)
    assert n % tm == 0 and tm % t == 0
    xf = x.reshape(n, d)
    xs = jnp.concatenate([jnp.zeros((1, d), F32), xf[:-1]], axis=0)
    s0x = jnp.concatenate([shift0[:, None, :], jnp.zeros((b, t - 1, d), F32)], axis=1).reshape(n, d)
    blk = pl.BlockSpec((tm, d), lambda i: (i, 0))
    out = pl.pallas_call(
        functools.partial(_rwkv_pre_flat_kernel, t),
        grid=(n // tm,),
        in_specs=[blk, blk, blk, _const_spec((1, d))] + [_const_spec(w.shape) for w in weights],
        out_specs=[blk] * 8,
        out_shape=[jax.ShapeDtypeStruct((n, d), F32)] * 8,
        compiler_params=_params(("parallel",)),
    )(xf, xs, s0x, gmix.reshape(1, d), *weights)
    seven = [o.reshape(b, t, d) for o in out[:7]]
    return seven, out[7].reshape(b, t, d)[:, t - 1, :]


def _wkv_chunk_kernel(nb, tc, c, heads, r_ref, w_ref, k_ref, v_ref, kk_ref, a_ref, g_ref,
                      rk_ref, lng_ref, lnb_ref, s0_ref, og_ref, sout_ref, h_scr):
    step = pl.program_id(1)
    n = RW_HEAD
    items = [(e, h) for e in range(nb) for h in range(heads)]

    @pl.when(step == 0)
    def _():
        for i, (e, h) in enumerate(items):
            h_scr[i] = _transpose_f32(s0_ref[e, h])

    ri = lax.broadcasted_iota(jnp.int32, (c, c), 0)
    ci = lax.broadcasted_iota(jnp.int32, (c, c), 1)
    tril_incl = ri >= ci
    tril_strict = ri > ci
    eye_c = (ri == ci).astype(F32)
    eye_n = (lax.broadcasted_iota(jnp.int32, (n, n), 0) == lax.broadcasted_iota(jnp.int32, (n, n), 1))
    zeros_cn = jnp.zeros((c, n), BF16)

    def token_terms(e):
        def load(ref):
            x = ref[e]
            if tc < c:
                x = jnp.concatenate([x, jnp.zeros((c - tc, x.shape[1]), F32)], axis=0)
            return x

        r, w, k, v, kk, a, g = (load(t) for t in (r_ref, w_ref, k_ref, v_ref, kk_ref, a_ref, g_ref))
        lw = _mm_exact_lhs(tril_incl, w)
        lw_end = lw[c - 1:c, :]
        p_inv = jnp.exp(-lw)
        p_tail = jnp.exp(lw_end - lw)
        kb = kk * a
        rt = r * jnp.exp(lw)
        return dict(
            v=v, g=g, rt=rt, p_end=jnp.exp(lw_end), bonus_rk=r * k * rk_ref[...],
            kt=(kk * jnp.exp(lw - w)).astype(BF16), rt_b=rt.astype(BF16), nbt=(-(kb * p_inv)).astype(BF16),
            kti=(k * p_inv).astype(BF16), nbh=(-(kb * p_tail)).astype(BF16), kh=(k * p_tail).astype(BF16),
            vb=v.astype(BF16))

    terms = [token_terms(e) for e in range(nb)]
    col = lambda name, e, h: terms[e][name][:, h * n:(h + 1) * n]

    aa = [_mm_nt(jnp.concatenate([col('kt', e, h), col('rt_b', e, h)], axis=0),
                 jnp.concatenate([col('nbt', e, h), col('kti', e, h)], axis=0)) for e, h in items]
    a_ub = [jnp.where(tril_strict, x[:c, :c], 0.0) for x in aa]
    a_uk = [jnp.where(tril_strict, x[:c, c:], 0.0).astype(BF16) for x in aa]
    a_r = [jnp.concatenate([jnp.where(tril_incl, x[c:, :c], 0.0), jnp.where(tril_incl, x[c:, c:], 0.0)],
                           axis=1).astype(BF16) for x in aa]
    tinv = [eye_c + x for x in a_ub]
    pw = [x.astype(BF16) for x in a_ub]
    span = 2
    while span < c:
        pw = [jnp.dot(x, x, preferred_element_type=F32).astype(BF16) for x in pw]
        tinv = [t + jnp.dot(x, t.astype(BF16), preferred_element_type=F32) for x, t in zip(pw, tinv)]
        span *= 2
    av = [jnp.dot(x, col('vb', e, h), preferred_element_type=F32) for x, (e, h) in zip(a_uk, items)]
    wu = [_mm(t, jnp.concatenate([col('kt', e, h), x.astype(BF16)], axis=1))
          for t, x, (e, h) in zip(tinv, av, items)]
    zt = [_mm_nt(eye_n, jnp.concatenate([col('nbh', e, h), col('kh', e, h)], axis=0))
          for e, h in items]
    res = [_mm(jnp.concatenate([x, z.astype(BF16)], axis=0),
               jnp.concatenate([y.astype(BF16), jnp.concatenate([zeros_cn, col('vb', e, h)], axis=1)], axis=0))
           for x, z, y, (e, h) in zip(a_r, zt, wu, items)]
    ro = [_mm(jnp.concatenate([col('rt', e, h) + x[:c, :n],
                               jnp.where(eye_n, col('p_end', e, h), 0.0) + x[c:, :n]], axis=0), h_scr[i])
          for i, (x, (e, h)) in enumerate(zip(res, items))]
    for i, (e, h) in enumerate(items):
        s = slice(h * n, (h + 1) * n)
        o = ro[i][:c] + res[i][:c, n:]
        h_scr[i] = ro[i][c:] + res[i][c:, n:]
        mean = jnp.mean(o, axis=-1, keepdims=True)
        var = jnp.mean(jnp.square(o - mean), axis=-1, keepdims=True)
        on = (o - mean) * lax.rsqrt(var + GN_EPS)
        bonus = jnp.sum(col('bonus_rk', e, h), axis=-1, keepdims=True) * col('v', e, h)
        out = (on * lng_ref[:, s] + lnb_ref[:, s] + bonus) * col('g', e, h)
        og_ref[e, :, s] = out[:tc]

    @pl.when(step == pl.num_programs(1) - 1)
    def _():
        for i, (e, h) in enumerate(items):
            sout_ref[e, h] = _transpose_f32(h_scr[i])


def _wkv_scan(seven, rk, lng, lnb, state0):
    b, t, d = seven[0].shape
    heads = d // RW_HEAD
    tc = min(SCAN_CHUNK, t)
    c = max(tc, SCAN_MIN_CHUNK)
    nb = SCAN_SEQS_PER_STEP if b % SCAN_SEQS_PER_STEP == 0 else 1
    assert t % tc == 0
    blk = pl.BlockSpec((nb, tc, d), lambda bi, j: (bi, j, 0))
    st = pl.BlockSpec((nb, heads, RW_HEAD, RW_HEAD), lambda bi, j: (bi, 0, 0, 0))
    og, s_out = pl.pallas_call(
        functools.partial(_wkv_chunk_kernel, nb, tc, c, heads),
        grid=(b // nb, t // tc),
        in_specs=[blk] * 7 + [_const_spec((1, d))] * 3 + [st],
        out_specs=[blk, st],
        out_shape=[jax.ShapeDtypeStruct((b, t, d), F32),
                   jax.ShapeDtypeStruct((b, heads, RW_HEAD, RW_HEAD), F32)],
        scratch_shapes=[pltpu.VMEM((nb * heads, RW_HEAD, RW_HEAD), F32)],
        compiler_params=_params(("parallel", "arbitrary")),
    )(*seven, rk.reshape(1, d).astype(F32), lng.reshape(1, d).astype(F32),
      lnb.reshape(1, d).astype(F32), state0)
    return og, s_out


def _wkv_lanes_kernel(steps, r_ref, w_ref, k_ref, v_ref, kk_ref, a_ref, g_ref, rk_ref, lng_ref, lnb_ref,
                      s0_ref, og_ref, sout_ref):
    s = s0_ref[0]
    for t in range(steps):
        kk = kk_ref[t]
        k = k_ref[t]
        v = v_ref[t]
        r = r_ref[t]
        sa = jnp.sum(s * kk[None], axis=1)
        s = s * jnp.exp(w_ref[t])[None] - sa[:, None, :] * (kk * a_ref[t])[None] + v[:, None, :] * k[None]
        o = jnp.sum(s * r[None], axis=1)
        mean = jnp.mean(o, axis=0, keepdims=True)
        var = jnp.mean(jnp.square(o - mean), axis=0, keepdims=True)
        on = (o - mean) * lax.rsqrt(var + GN_EPS)
        bonus = jnp.sum(r * k * rk_ref[...], axis=0, keepdims=True) * v
        og_ref[t] = (on * lng_ref[...] + lnb_ref[...] + bonus) * g_ref[t]
    sout_ref[0] = s


def _wkv_lanes(seven, rk, lng, lnb, state0_t):
    b, t, d = seven[0].shape
    heads = d // RW_HEAD
    tok = [jnp.transpose(x, (1, 2, 0)) for x in seven]
    col = lambda p: jnp.broadcast_to(p.reshape(d, 1).astype(F32), (d, b))
    blk = pl.BlockSpec((t, RW_HEAD, b), lambda h: (0, h, 0))
    par = pl.BlockSpec((RW_HEAD, b), lambda h: (h, 0))
    st = pl.BlockSpec((1, RW_HEAD, RW_HEAD, b), lambda h: (h, 0, 0, 0))
    og, s_out = pl.pallas_call(
        functools.partial(_wkv_lanes_kernel, t),
        grid=(heads,),
        in_specs=[blk] * 7 + [par] * 3 + [st],
        out_specs=[blk, st],
        out_shape=[jax.ShapeDtypeStruct((t, d, b), F32), jax.ShapeDtypeStruct(state0_t.shape, F32)],
        compiler_params=_params(("parallel",)),
    )(*tok, col(rk), col(lng), col(lnb), state0_t)
    return jnp.transpose(og, (2, 0, 1)), s_out


def _post_kernel(mla_heads, final_norm, mix_ref, x_ref, *rest):
    if mla_heads:
        wuv_ref, rest = rest[0], rest[1:]
    wo_ref, nf_ref, up_ref, down_ref = rest[:4]
    rest = rest[4:]
    if final_norm:
        fn_ref, rest = rest[0], rest[1:]
    y_ref = rest[0]
    mix = mix_ref[...]
    if mla_heads:
        rank = mix.shape[1] // mla_heads
        mix = jnp.concatenate(
            [_mm(mix[:, h * rank:(h + 1) * rank], wuv_ref[h]) for h in range(mla_heads)], axis=1)
    x2 = x_ref[...] + _mm(mix, wo_ref[...])
    hn = _rms(x2, nf_ref[...]).astype(BF16)
    acc = x2
    dff = up_ref.shape[1]
    fc = min(FF_CHUNK, dff)
    for cidx in range(dff // fc):
        hid = jnp.dot(hn, up_ref[:, cidx * fc:(cidx + 1) * fc], preferred_element_type=F32)
        hid = jnp.square(jnp.maximum(hid, 0.0)).astype(BF16)
        acc = acc + jnp.dot(hid, down_ref[cidx * fc:(cidx + 1) * fc, :], preferred_element_type=F32)
    if final_norm:
        acc = _rms(acc, fn_ref[...])
    y_ref[...] = acc


def _post(mix, x, wo, nf, up, down, wuv=None, fn=None):
    n, d = x.shape
    tm = min(ROW_TILE, n)
    assert n % tm == 0
    ins = [mix, x]
    specs = [pl.BlockSpec((tm, mix.shape[1]), lambda i: (i, 0)), pl.BlockSpec((tm, d), lambda i: (i, 0))]

    def add_const(arr):
        ins.append(arr)
        specs.append(pl.BlockSpec(arr.shape, lambda i, _nd=arr.ndim: (0,) * _nd,
                                  pipeline_mode=pl.Buffered(1)))

    if wuv is not None:
        add_const(wuv)
    for arr in (wo, nf.reshape(1, d).astype(F32), up, down):
        add_const(arr)
    if fn is not None:
        add_const(fn.reshape(1, d).astype(F32))
    return pl.pallas_call(
        functools.partial(_post_kernel, 0 if wuv is None else wuv.shape[0], fn is not None),
        grid=(n // tm,),
        in_specs=specs,
        out_specs=pl.BlockSpec((tm, d), lambda i: (i, 0)),
        out_shape=jax.ShapeDtypeStruct((n, d), F32),
        compiler_params=_params(("parallel",)),
    )(*ins)


def _mla_pre_kernel(heads, scale, x_ref, cos_ref, sin_ref, kvn_ref, wdkv_ref, latn_ref, wkr_ref, wkrp_ref,
                    nmix_ref, wdq_ref, qn_ref, wuqn_ref, wuqr_ref, wuqrp_ref, wukt_ref,
                    lat_ref, kr_ref, kc_ref, qc_ref):
    x = x_ref[...]
    cos = cos_ref[...]
    sin = sin_ref[...]
    rows = x.shape[0]
    rank = wukt_ref.shape[2]
    kw = kc_ref.shape[1]
    pad = kw - rank - QK_ROPE
    kv_in = _rms(x, kvn_ref[...]).astype(BF16)
    lat = _rms(jnp.dot(kv_in, wdkv_ref[...], preferred_element_type=F32), latn_ref[...])
    kr = (jnp.dot(kv_in, wkr_ref[...], preferred_element_type=F32) * cos[:, :QK_ROPE]
          + jnp.dot(kv_in, wkrp_ref[...], preferred_element_type=F32) * sin[:, :QK_ROPE])
    lat_ref[...] = lat
    kr_ref[...] = kr
    kc_ref[:, :rank] = lat.astype(BF16)
    kc_ref[:, rank:rank + QK_ROPE] = kr.astype(BF16)
    if pad:
        kc_ref[:, rank + QK_ROPE:] = jnp.zeros((rows, pad), BF16)
    xn = _rms(x, nmix_ref[...]).astype(BF16)
    cq = _rms(jnp.dot(xn, wdq_ref[...], preferred_element_type=F32), qn_ref[...]).astype(BF16)
    qn = jnp.dot(cq, wuqn_ref[...], preferred_element_type=F32)
    qr = ((jnp.dot(cq, wuqr_ref[...], preferred_element_type=F32) * cos
           + jnp.dot(cq, wuqrp_ref[...], preferred_element_type=F32) * sin) * scale).astype(BF16)
    for h in range(heads):
        ql = _mm(qn[:, h * QK_NOPE:(h + 1) * QK_NOPE], wukt_ref[h])
        qc_ref[:, h * kw:h * kw + rank] = (ql * scale).astype(BF16)
        qc_ref[:, h * kw + rank:h * kw + rank + QK_ROPE] = qr[:, h * QK_ROPE:(h + 1) * QK_ROPE]
        if pad:
            qc_ref[:, h * kw + rank + QK_ROPE:(h + 1) * kw] = jnp.zeros((rows, pad), BF16)


def _rope_tables(pos, heads):
    half = QK_ROPE // 2
    inv_freq = ROPE_BASE ** (-jnp.arange(half, dtype=F32) / half)
    ang = pos.astype(F32)[:, None] * inv_freq[None, :]
    cos, sin = jnp.cos(ang), jnp.sin(ang)
    cos2 = jnp.concatenate([cos, cos], axis=1)
    sin2 = jnp.concatenate([-sin, sin], axis=1)
    return jnp.tile(cos2, (1, heads)), jnp.tile(sin2, (1, heads))


def _swap_halves(wcols):
    half = QK_ROPE // 2
    return jnp.concatenate([wcols[..., half:], wcols[..., :half]], axis=-1)


def _mla_weights(p, j):
    d, rank = p['w_dkv'].shape
    heads = p['w_uk'].shape[1]
    qrank = p['w_dq'].shape[2]
    wuq = p['w_uq'][j]
    wuq_n = wuq[:, :, :QK_NOPE].reshape(qrank, heads * QK_NOPE)
    wuq_r = wuq[:, :, QK_NOPE:]
    row2 = lambda t: t.reshape(1, -1).astype(F32)
    return dict(
        kvn=row2(p['kv_norm']), wdkv=p['w_dkv'].astype(BF16), latn=row2(p['lat_norm']),
        wkr=p['w_kr'].astype(BF16), wkrp=_swap_halves(p['w_kr']).astype(BF16),
        nmix=row2(p['norm_mix'][p['rw_wr'].shape[0] + j]), wdq=p['w_dq'][j].astype(BF16), qn=row2(p['q_norm'][j]),
        wuqn=wuq_n.astype(BF16), wuqr=wuq_r.reshape(qrank, heads * QK_ROPE).astype(BF16),
        wuqrp=_swap_halves(wuq_r).reshape(qrank, heads * QK_ROPE).astype(BF16),
        wukt=jnp.transpose(p['w_uk'], (1, 2, 0)).astype(BF16),
        wuv=jnp.transpose(p['w_uv'], (1, 0, 2)).astype(BF16),
        heads=heads, rank=rank)


def _key_width(rank):
    return -(-(rank + QK_ROPE) // 128) * 128


def _mla_pre(x, cos, sin, mw):
    n, d = x.shape
    heads, rank = mw['heads'], mw['rank']
    kw = _key_width(rank)
    tm = min(ROW_TILE, n)
    assert n % tm == 0 and cos.shape[0] % tm == 0
    tab_blocks = cos.shape[0] // tm
    row = lambda width: pl.BlockSpec((tm, width), lambda i: (i, 0))
    tab = pl.BlockSpec((tm, heads * QK_ROPE), lambda i: (i % tab_blocks, 0))
    names = ('kvn', 'wdkv', 'latn', 'wkr', 'wkrp', 'nmix', 'wdq', 'qn', 'wuqn', 'wuqr', 'wuqrp', 'wukt')
    consts = [mw[k] for k in names]
    scale = float((QK_NOPE + QK_ROPE) ** -0.5)
    return pl.pallas_call(
        functools.partial(_mla_pre_kernel, heads, scale),
        grid=(n // tm,),
        in_specs=[row(d), tab, tab] + [_const_spec(a.shape) for a in consts],
        out_specs=[row(rank), row(QK_ROPE), row(kw), row(heads * kw)],
        out_shape=[jax.ShapeDtypeStruct((n, rank), F32), jax.ShapeDtypeStruct((n, QK_ROPE), F32),
                   jax.ShapeDtypeStruct((n, kw), BF16), jax.ShapeDtypeStruct((n, heads * kw), BF16)],
        compiler_params=_params(("parallel",)),
    )(x, cos, sin, *consts)


def _softmax_update(s, vals, m_scr, l_scr, acc_scr):
    lanes = m_scr.shape[1]
    width = s.shape[1]
    m_prev = m_scr[...]
    m_new = jnp.maximum(m_prev, jnp.max(s, axis=-1, keepdims=True))
    alpha = jnp.exp(m_prev - m_new)
    m_wide = m_new[:, :width] if width <= lanes else jnp.tile(m_new, (1, width // lanes))
    p = jnp.exp(s - m_wide)
    l_scr[...] = alpha * l_scr[...] + jnp.sum(p, axis=-1, keepdims=True)
    acc_scr[...] = (jnp.tile(alpha, (1, acc_scr.shape[1] // lanes)) * acc_scr[...]
                    + jnp.dot(p.astype(BF16), vals, preferred_element_type=F32))
    m_scr[...] = m_new


def _softmax_init(m_scr, l_scr, acc_scr):
    m_scr[...] = jnp.full(m_scr.shape, MASK_VALUE, F32)
    l_scr[...] = jnp.zeros(l_scr.shape, F32)
    acc_scr[...] = jnp.zeros(acc_scr.shape, F32)


def _softmax_result(l_scr, acc_scr):
    return acc_scr[...] / jnp.tile(l_scr[...], (1, acc_scr.shape[1] // l_scr.shape[1]))


def _attn_prompt_kernel(heads, rank, tk, qc_ref, kc_ref, pkc_ref, o_ref, m_scr, l_scr, acc_scr):
    i = pl.program_id(1)
    tq = qc_ref.shape[0]
    kw = kc_ref.shape[2]
    qc = jnp.concatenate([qc_ref[:, h * kw:(h + 1) * kw] for h in range(heads)], axis=0)
    _softmax_init(m_scr, l_scr, acc_scr)
    pkc = pkc_ref[...]
    _softmax_update(_mm_nt(qc, pkc), pkc[:, :rank], m_scr, l_scr, acc_scr)
    n_full = (i * tq) // tk

    def body(jb, carry):
        kc = kc_ref[0, pl.ds(pl.multiple_of(jb * tk, tk), tk), :]
        _softmax_update(_mm_nt(qc, kc), kc[:, :rank], m_scr, l_scr, acc_scr)
        return carry

    lax.fori_loop(0, n_full, body, 0)
    kc = kc_ref[0, pl.ds(pl.multiple_of(n_full * tk, tk), tk), :]
    q_pos = i * tq + lax.broadcasted_iota(jnp.int32, (heads * tq, tk), 0) % tq
    k_pos = n_full * tk + lax.broadcasted_iota(jnp.int32, (heads * tq, tk), 1)
    s = jnp.where(k_pos <= q_pos, _mm_nt(qc, kc), MASK_VALUE)
    _softmax_update(s, kc[:, :rank], m_scr, l_scr, acc_scr)
    o = _softmax_result(l_scr, acc_scr).astype(BF16)
    for h in range(heads):
        o_ref[:, h * rank:(h + 1) * rank] = o[h * tq:(h + 1) * tq, :]


def _attn_prompt(qc, kc, pkc, batch, heads, rank):
    n = qc.shape[0]
    t = n // batch
    kw = kc.shape[1]
    tq = min(ATTN_TILE, t)
    tk = min(ATTN_KEY_TILE, t)
    assert t % tk == 0 and tk % tq == 0
    nq = t // tq
    return pl.pallas_call(
        functools.partial(_attn_prompt_kernel, heads, rank, tk),
        grid=(batch, nq),
        in_specs=[pl.BlockSpec((tq, heads * kw), lambda b, i: (b * nq + i, 0)),
                  pl.BlockSpec((1, t, kw), lambda b, i: (b, 0, 0)),
                  _const_spec(pkc.shape)],
        out_specs=pl.BlockSpec((tq, heads * rank), lambda b, i: (b * nq + i, 0)),
        out_shape=jax.ShapeDtypeStruct((n, heads * rank), BF16),
        scratch_shapes=[pltpu.VMEM((heads * tq, 128), F32), pltpu.VMEM((heads * tq, 128), F32),
                        pltpu.VMEM((heads * tq, rank), F32)],
        compiler_params=_params(("parallel", "arbitrary")),
    )(qc, kc.reshape(batch, t, kw), pkc)


def _attn_sample_kernel(chunk_pages, seq, rank, pt_ref, qc_ref, nkc_ref, lat_hbm, krt_hbm, o_ref,
                        lat_buf, krt_buf, sems, m_scr, l_scr, acc_scr):
    b = pl.program_id(0)
    qc = qc_ref[0]

    def page_copies(seq_idx, chunk, slot, p):
        page = pt_ref[seq_idx, chunk * chunk_pages + p]
        rows = pl.ds(pl.multiple_of(p * PAGE_SIZE, PAGE_SIZE), PAGE_SIZE)
        return (pltpu.make_async_copy(lat_hbm.at[page], lat_buf.at[slot, rows, :], sems.at[slot, 0]),
                pltpu.make_async_copy(krt_hbm.at[page], krt_buf.at[slot, :, rows], sems.at[slot, 1]))

    def start_chunk(seq_idx, chunk, slot):
        def body(p, carry):
            for cp in page_copies(seq_idx, chunk, slot, p):
                cp.start()
            return carry
        lax.fori_loop(0, chunk_pages, body, 0)

    def wait_chunk(seq_idx, chunk, slot):
        def body(p, carry):
            for cp in page_copies(seq_idx, chunk, slot, p):
                cp.wait()
            return carry
        lax.fori_loop(0, chunk_pages, body, 0)

    @pl.when(b == 0)
    def _():
        start_chunk(0, 0, 0)

    _softmax_init(m_scr, l_scr, acc_scr)
    for chunk in range(SAMPLE_CHUNKS):
        slot = chunk % 2
        wait_chunk(b, chunk, slot)
        if chunk + 1 < SAMPLE_CHUNKS:
            start_chunk(b, chunk + 1, 1 - slot)
        else:
            @pl.when(b + 1 < pl.num_programs(0))
            def _():
                start_chunk(b + 1, 0, 1 - slot)
        kl = lat_buf[slot].astype(BF16)
        krt = krt_buf[slot].astype(BF16)
        s = _mm_nt(qc[:, :rank], kl) + jnp.dot(qc[:, rank:rank + QK_ROPE], krt, preferred_element_type=F32)
        _softmax_update(s, kl, m_scr, l_scr, acc_scr)

    nkc = nkc_ref[0]
    s = _mm_nt(qc, nkc)
    q_t = lax.broadcasted_iota(jnp.int32, s.shape, 0) % seq
    k_t = lax.broadcasted_iota(jnp.int32, s.shape, 1)
    s = jnp.where(k_t <= q_t, s, MASK_VALUE)
    _softmax_update(s, nkc[:, :rank], m_scr, l_scr, acc_scr)
    o_ref[0] = _softmax_result(l_scr, acc_scr).astype(BF16)


def _attn_sample(qc, nkc, seq, rank, cache_latent, cache_krope_t, page_table):
    b, rows, kw = qc.shape
    total_pages = page_table.shape[1]
    assert SAMPLE_CHUNKS % 2 == 0 and total_pages % SAMPLE_CHUNKS == 0
    chunk_pages = total_pages // SAMPLE_CHUNKS
    chunk_keys = chunk_pages * PAGE_SIZE
    per_b = lambda arr: pl.BlockSpec((1,) + arr.shape[1:], lambda bi, pt: (bi, 0, 0))
    grid_spec = pltpu.PrefetchScalarGridSpec(
        num_scalar_prefetch=1,
        grid=(b,),
        in_specs=[per_b(qc), per_b(nkc), pl.BlockSpec(memory_space=pl.ANY), pl.BlockSpec(memory_space=pl.ANY)],
        out_specs=pl.BlockSpec((1, rows, rank), lambda bi, pt: (bi, 0, 0)),
        scratch_shapes=[pltpu.VMEM((2, chunk_keys, rank), F32), pltpu.VMEM((2, QK_ROPE, chunk_keys), F32),
                        pltpu.SemaphoreType.DMA((2, 2)),
                        pltpu.VMEM((rows, 128), F32), pltpu.VMEM((rows, 128), F32), pltpu.VMEM((rows, rank), F32)],
    )
    return pl.pallas_call(
        functools.partial(_attn_sample_kernel, chunk_pages, seq, rank),
        grid_spec=grid_spec,
        out_shape=jax.ShapeDtypeStruct((b, rows, rank), BF16),
        compiler_params=_params(("arbitrary",)),
    )(page_table, qc, nkc, cache_latent, cache_krope_t)


def _layer0(x, shift0, wkv0, p, flat):
    b, t, d = x.shape
    weights = _rwkv_pre_weights(p, 0)
    pre = _rwkv_pre_flat if flat else _rwkv_pre_seq
    seven, shift_new = pre(x, shift0, p['norm_mix'][0], weights)
    scan = _wkv_lanes if flat else _wkv_scan
    og, wkv_new = scan(seven, p['rw_rk'][0], p['rw_lnx_g'][0], p['rw_lnx_b'][0], wkv0)
    x1 = _post(og.reshape(b * t, d), x.reshape(b * t, d), p['rw_wo'][0].astype(BF16), p['norm_ffn'][0],
               p['ffn_up'][0].astype(BF16), p['ffn_down'][0].astype(BF16))
    return x1, wkv_new, shift_new


def _layer1_post(olat, x1, p, mw):
    return _post(olat, x1, p['w_o_mla'][0].astype(BF16), p['norm_ffn'][1], p['ffn_up'][1].astype(BF16),
                 p['ffn_down'][1].astype(BF16), wuv=mw['wuv'], fn=p['norm_final'])


def kernel(x_prompt, x_sample, state_wkv, state_shift, cache_latent, cache_krope, page_table,
           meta_tokens, rw_mu, rw_wr, rw_wk, rw_wv, rw_wo, rw_w0, rw_w1, rw_w2, rw_a0, rw_a1, rw_a2,
           rw_g1, rw_g2, rw_kk, rw_ka, rw_rk, rw_lnx_g, rw_lnx_b, norm_mix, norm_ffn, ffn_up, ffn_down,
           kv_norm, w_dkv, lat_norm, w_kr, w_uk, w_uv, w_dq, q_norm, w_uq, w_o_mla, norm_final):
    p = dict(rw_mu=rw_mu, rw_wr=rw_wr, rw_wk=rw_wk, rw_wv=rw_wv, rw_wo=rw_wo, rw_w0=rw_w0, rw_w1=rw_w1,
             rw_w2=rw_w2, rw_a0=rw_a0, rw_a1=rw_a1, rw_a2=rw_a2, rw_g1=rw_g1, rw_g2=rw_g2, rw_kk=rw_kk,
             rw_ka=rw_ka, rw_rk=rw_rk, rw_lnx_g=rw_lnx_g, rw_lnx_b=rw_lnx_b, norm_mix=norm_mix,
             norm_ffn=norm_ffn, ffn_up=ffn_up, ffn_down=ffn_down, kv_norm=kv_norm, w_dkv=w_dkv,
             lat_norm=lat_norm, w_kr=w_kr, w_uk=w_uk, w_uv=w_uv, w_dq=w_dq, q_norm=q_norm, w_uq=w_uq,
             w_o_mla=w_o_mla, norm_final=norm_final)
    assert rw_wr.shape[0] == 1 and w_dq.shape[0] == 1, "one RWKV layer followed by one MLA layer"
    bsz, seq, d = x_prompt.shape
    dbsz, dseq, _ = x_sample.shape
    rw_heads = d // RW_HEAD
    mw = _mla_weights(p, 0)
    heads, rank = mw['heads'], mw['rank']
    n_meta = meta_tokens.shape[0]
    past_len = page_table.shape[1] * PAGE_SIZE

    xm1, wkv_m, shift_m = _layer0(meta_tokens[None].astype(F32), jnp.zeros((1, d), F32),
                                  jnp.zeros((1, rw_heads, RW_HEAD, RW_HEAD), F32), p, flat=False)
    cos_m, sin_m = _rope_tables(jnp.arange(n_meta, dtype=jnp.int32), heads)
    lat_m, kr_m, kc_m, _ = _mla_pre(xm1, cos_m, sin_m, mw)

    xp1, wkv_p, shift_p = _layer0(x_prompt, jnp.broadcast_to(shift_m, (bsz, d)),
                                  jnp.broadcast_to(wkv_m, (bsz,) + wkv_m.shape[1:]), p, flat=False)
    cos_p, sin_p = _rope_tables(n_meta + jnp.arange(seq, dtype=jnp.int32), heads)
    lat_p, kr_p, kc_p, qc_p = _mla_pre(xp1, cos_p, sin_p, mw)
    olat_p = _attn_prompt(qc_p, kc_p, kc_m, bsz, heads, rank)
    y_prompt = _layer1_post(olat_p, xp1, p, mw).reshape(bsz, seq, d)
    latent_prompt = jnp.concatenate(
        [jnp.broadcast_to(lat_m[None], (bsz, n_meta, rank)), lat_p.reshape(bsz, seq, rank)], axis=1)
    krope_prompt = jnp.concatenate(
        [jnp.broadcast_to(kr_m[None], (bsz, n_meta, QK_ROPE)), kr_p.reshape(bsz, seq, QK_ROPE)], axis=1)

    xs1, wkv_s, shift_s = _layer0(x_sample, state_shift[0], jnp.transpose(state_wkv[0], (1, 2, 3, 0)), p, flat=True)
    wkv_s = jnp.transpose(wkv_s, (3, 0, 1, 2))
    cos_s, sin_s = _rope_tables(past_len + jnp.arange(dseq, dtype=jnp.int32), heads)
    cos_s, sin_s = jnp.tile(cos_s, (dbsz, 1)), jnp.tile(sin_s, (dbsz, 1))
    lat_s, kr_s, kc_s, qc_s = _mla_pre(xs1, cos_s, sin_s, mw)
    kw = kc_s.shape[1]
    qc_s = jnp.transpose(qc_s.reshape(dbsz, dseq, heads, kw), (0, 2, 1, 3)).reshape(dbsz, heads * dseq, kw)
    new_rows = -(-dseq // 16) * 16
    nkc_s = jnp.pad(kc_s.reshape(dbsz, dseq, kw), ((0, 0), (0, new_rows - dseq), (0, 0)))
    olat_s = _attn_sample(qc_s, nkc_s, dseq, rank, cache_latent, jnp.swapaxes(cache_krope, 1, 2), page_table)
    olat_s = jnp.transpose(olat_s.reshape(dbsz, heads, dseq, rank), (0, 2, 1, 3)).reshape(dbsz * dseq, heads * rank)
    y_sample = _layer1_post(olat_s, xs1, p, mw).reshape(dbsz, dseq, d)

    return (y_prompt, y_sample, wkv_p[None], shift_p[None], latent_prompt, krope_prompt,
            wkv_s[None], shift_s[None], lat_s.reshape(dbsz, dseq, rank), kr_s.reshape(dbsz, dseq, QK_ROPE))
```

```python
import functools
import math

import jax
import jax.numpy as jnp
from jax import lax
from jax.experimental import pallas as pl
from jax.experimental.pallas import tpu as pltpu

F32 = jnp.float32
BF16 = jnp.bfloat16

RW_HEAD = 64
QK_NOPE = 128
QK_ROPE = 64
V_HEAD = 128
N_META = 16
PAGE_SIZE = 128
NORM_EPS = 1e-6
GN_EPS = 64e-5
ROPE_BASE = 10000.0
DECAY_SCALE = math.exp(-0.5)
MASK_VALUE = -1e30

V7X_VMEM_LIMIT_BYTES = 56 * 1024 * 1024
SCAN_CHUNK = 64
SCAN_MIN_CHUNK = 16
SCAN_SEQS_PER_STEP = 2
ROW_TILE = 256
FF_CHUNK = 1024
ATTN_TILE = 128
ATTN_KEY_TILE = 512
ATTN_ROW_PARTS = 2
SAMPLE_CHUNKS = 2


def _mm(a, b):
    return jnp.dot(a.astype(BF16), b.astype(BF16), preferred_element_type=F32)


def _mm_nt(a, b):
    return lax.dot_general(a.astype(BF16), b.astype(BF16), (((1,), (1,)), ((), ())),
                           preferred_element_type=F32)


def _split3(x):
    h = x.astype(BF16)
    r1 = x - h.astype(F32)
    m = r1.astype(BF16)
    lo = (r1 - m.astype(F32)).astype(BF16)
    return h, m, lo


def _mm_exact_lhs(sel, x):
    h, m, lo = _split3(x)
    sel = sel.astype(BF16)
    dot = lambda t: jnp.dot(sel, t, preferred_element_type=F32)
    return dot(h) + (dot(m) + dot(lo))


def _transpose_f32(x):
    n = x.shape[1]
    eye = (lax.broadcasted_iota(jnp.int32, (n, n), 0) == lax.broadcasted_iota(jnp.int32, (n, n), 1))
    eye = eye.astype(BF16)
    h, m, lo = _split3(x)
    dot = lambda t: lax.dot_general(eye, t, (((1,), (1,)), ((), ())), preferred_element_type=F32)
    return dot(h) + (dot(m) + dot(lo))


def _rms(x, g):
    return x * lax.rsqrt(jnp.mean(x * x, axis=-1, keepdims=True) + NORM_EPS) * g


def _sigmoid(z):
    return 1.0 / (1.0 + jnp.exp(-z))


def _const_spec(shape):
    nd = len(shape)
    return pl.BlockSpec(shape, lambda *_: (0,) * nd)


def _params(sem):
    return pltpu.CompilerParams(dimension_semantics=sem, vmem_limit_bytes=V7X_VMEM_LIMIT_BYTES)


def _rwkv_pre_math(xn, prev, w, outs):
    (mu, wr, wk, wv, w0, w1, w2, a0, a1, a2, g1, g2, kkp, kap, hsum, hbc) = w
    r_ref, w_ref, k_ref, v_ref, kk_ref, a_ref, g_ref = outs
    xx = prev - xn
    mix = lambda m: (xn + xx * mu[m:m + 1, :]).astype(BF16)
    r = _mm(mix(0), wr[...])
    z = w0[...] + _mm(jnp.tanh(_mm(mix(1), w1[...])), w2[...])
    wdec = -DECAY_SCALE * _sigmoid(z)
    k = _mm(mix(2), wk[...])
    v = _mm(mix(3), wv[...])
    a = _sigmoid(a0[...] + _mm(_mm(mix(4), a1[...]), a2[...]))
    g = _mm(_sigmoid(_mm(mix(5), g1[...])), g2[...])
    kkr = k * kkp[...]
    sq = kkr * kkr
    sq_h = sq.astype(BF16)
    sq_l = (sq - sq_h.astype(F32)).astype(BF16)
    ssq = (jnp.dot(sq_h, hsum[...], preferred_element_type=F32)
           + jnp.dot(sq_l, hsum[...], preferred_element_type=F32))
    inv = jnp.minimum(lax.rsqrt(ssq), 1e12)
    inv_h = inv.astype(BF16)
    inv_l = (inv - inv_h.astype(F32)).astype(BF16)
    inv_b = (jnp.dot(inv_h, hbc[...], preferred_element_type=F32)
             + jnp.dot(inv_l, hbc[...], preferred_element_type=F32))
    r_ref[...] = r.reshape(r_ref.shape)
    w_ref[...] = wdec.reshape(w_ref.shape)
    k_ref[...] = (k * (1.0 + (a - 1.0) * kap[...])).reshape(k_ref.shape)
    v_ref[...] = v.reshape(v_ref.shape)
    kk_ref[...] = (kkr * inv_b).reshape(kk_ref.shape)
    a_ref[...] = a.reshape(a_ref.shape)
    g_ref[...] = g.reshape(g_ref.shape)


def _rwkv_pre_seq_kernel(x_ref, halo_ref, s0_ref, gmix_ref, *rest):
    w, outs, tail_ref = rest[:16], rest[16:23], rest[23]
    j = pl.program_id(1)
    x = x_ref[0]
    tm = x.shape[0]
    gm = gmix_ref[...]
    xn = _rms(x, gm)
    row = lax.broadcasted_iota(jnp.int32, (tm, 1), 0)
    prev_raw = jnp.where(row == 0, halo_ref[0][7:8, :], pltpu.roll(x, 1, 0))
    prev = jnp.where((row == 0) & (j == 0), s0_ref[0], _rms(prev_raw, gm))
    tail_ref[0] = xn[tm - 8:, :]
    _rwkv_pre_math(xn, prev, w, outs)


def _rwkv_pre_flat_kernel(seq_len, x_ref, xs_ref, s0x_ref, gmix_ref, *rest):
    w, outs, xn_ref = rest[:16], rest[16:23], rest[23]
    x = x_ref[...]
    tm = x.shape[0]
    gm = gmix_ref[...]
    xn = _rms(x, gm)
    row = lax.broadcasted_iota(jnp.int32, (tm, 1), 0)
    prev = jnp.where(row % seq_len == 0, s0x_ref[...], _rms(xs_ref[...], gm))
    xn_ref[...] = xn
    _rwkv_pre_math(xn, prev, w, outs)


def _rwkv_pre_weights(p, i):
    d = p['rw_wr'].shape[-1]
    heads = d // RW_HEAD
    head_of = jnp.arange(d, dtype=jnp.int32) // RW_HEAD
    hsum = (head_of[:, None] == jnp.arange(heads, dtype=jnp.int32)[None, :]).astype(BF16)
    row2 = lambda t: t.reshape(1, -1).astype(F32)
    return (p['rw_mu'][i].astype(F32), p['rw_wr'][i].astype(BF16), p['rw_wk'][i].astype(BF16),
            p['rw_wv'][i].astype(BF16), row2(p['rw_w0'][i]), p['rw_w1'][i].astype(BF16),
            p['rw_w2'][i].astype(BF16), row2(p['rw_a0'][i]), p['rw_a1'][i].astype(BF16),
            p['rw_a2'][i].astype(BF16), p['rw_g1'][i].astype(BF16), p['rw_g2'][i].astype(BF16),
            row2(p['rw_kk'][i]), row2(p['rw_ka'][i]), hsum, hsum.T)


def _rwkv_pre_seq(x, shift0, gmix, weights):
    b, t, d = x.shape
    tm = min(ROW_TILE, t)
    assert t % tm == 0 and tm % 8 == 0
    blk = pl.BlockSpec((1, tm, d), lambda bi, j: (bi, j, 0))
    halo = pl.BlockSpec((1, 8, d), lambda bi, j: (bi, jnp.maximum(j * (tm // 8) - 1, 0), 0))
    s0 = pl.BlockSpec((1, 1, d), lambda bi, j: (bi, 0, 0))
    tail = pl.BlockSpec((1, 8, d), lambda bi, j: (bi, 0, 0))
    out = pl.pallas_call(
        _rwkv_pre_seq_kernel,
        grid=(b, t // tm),
        in_specs=[blk, halo, s0, _const_spec((1, d))] + [_const_spec(w.shape) for w in weights],
        out_specs=[blk] * 7 + [tail],
        out_shape=[jax.ShapeDtypeStruct((b, t, d), F32)] * 7 + [jax.ShapeDtypeStruct((b, 8, d), F32)],
        compiler_params=_params(("parallel", "arbitrary")),
    )(x, x, shift0.reshape(b, 1, d), gmix.reshape(1, d), *weights)
    return out[:7], out[7][:, 7, :]


def _rwkv_pre_flat(x, shift0, gmix, weights):
    b, t, d = x.shape
    n = b * t
    tm = min(ROW_TILE, n)
    assert n % tm == 0 and tm % t == 0
    xf = x.reshape(n, d)
    xs = jnp.concatenate([jnp.zeros((1, d), F32), xf[:-1]], axis=0)
    s0x = jnp.concatenate([shift0[:, None, :], jnp.zeros((b, t - 1, d), F32)], axis=1).reshape(n, d)
    blk = pl.BlockSpec((tm, d), lambda i: (i, 0))
    out = pl.pallas_call(
        functools.partial(_rwkv_pre_flat_kernel, t),
        grid=(n // tm,),
        in_specs=[blk, blk, blk, _const_spec((1, d))] + [_const_spec(w.shape) for w in weights],
        out_specs=[blk] * 8,
        out_shape=[jax.ShapeDtypeStruct((n, d), F32)] * 8,
        compiler_params=_params(("parallel",)),
    )(xf, xs, s0x, gmix.reshape(1, d), *weights)
    seven = [o.reshape(b, t, d) for o in out[:7]]
    return seven, out[7].reshape(b, t, d)[:, t - 1, :]


def _wkv_chunk_kernel(nb, tc, c, heads, r_ref, w_ref, k_ref, v_ref, kk_ref, a_ref, g_ref,
                      rk_ref, lng_ref, lnb_ref, s0_ref, og_ref, sout_ref, h_scr):
    step = pl.program_id(1)
    n = RW_HEAD
    pw_lanes = 2 * n
    npairs = heads // 2
    pairs = [(e, p) for e in range(nb) for p in range(npairs)]
    items = [(e, p, hh) for e, p in pairs for hh in range(2)]
    zeros_nn = jnp.zeros((n, n), F32)

    @pl.when(step == 0)
    def _():
        for i, (e, p) in enumerate(pairs):
            t0 = _transpose_f32(s0_ref[e, 2 * p])
            t1 = _transpose_f32(s0_ref[e, 2 * p + 1])
            h_scr[i] = jnp.concatenate([jnp.concatenate([t0, zeros_nn], axis=1),
                                        jnp.concatenate([zeros_nn, t1], axis=1)], axis=0)

    iota = lambda shape, dim: lax.broadcasted_iota(jnp.int32, shape, dim)
    tril_incl = iota((c, c), 0) >= iota((c, c), 1)
    tril_strict = iota((c, c), 0) > iota((c, c), 1)
    eye_c = (iota((c, c), 0) == iota((c, c), 1)).astype(F32)
    uk_mask = (iota((c, 2 * c), 1) >= c) & (iota((c, 2 * c), 0) > iota((c, 2 * c), 1) - c)
    ar_mask = iota((c, 2 * c), 0) >= iota((c, 2 * c), 1) % c
    head_of = lambda rows: iota((rows, pw_lanes), 1) >= n
    second_wide = lambda rows: iota((rows, 2 * pw_lanes), 1) % pw_lanes >= n
    eye_pair = iota((pw_lanes, pw_lanes), 0) == iota((pw_lanes, pw_lanes), 1)
    pick = [iota((n, pw_lanes), 1) == iota((n, pw_lanes), 0) + n * hh for hh in range(2)]
    zeros_c = jnp.zeros((c, pw_lanes), BF16)

    def token_terms(e):
        def load(ref):
            x = ref[e]
            if tc < c:
                x = jnp.concatenate([x, jnp.zeros((c - tc, x.shape[1]), F32)], axis=0)
            return x

        r, w, k, v, kk, a, g = (load(t) for t in (r_ref, w_ref, k_ref, v_ref, kk_ref, a_ref, g_ref))
        lw = _mm_exact_lhs(tril_incl, w)
        lw_end = lw[c - 1:c, :]
        p_inv = jnp.exp(-lw)
        p_tail = jnp.exp(lw_end - lw)
        kb = kk * a
        rt = r * jnp.exp(lw)
        return dict(
            v=v, g=g, rt=rt, p_end=jnp.exp(lw_end), bonus_rk=r * k * rk_ref[...],
            kt=(kk * jnp.exp(lw - w)).astype(BF16), rt_b=rt.astype(BF16), nbt=(-(kb * p_inv)).astype(BF16),
            kti=(k * p_inv).astype(BF16), nbh=(-(kb * p_tail)).astype(BF16), kh=(k * p_tail).astype(BF16),
            vb=v.astype(BF16))

    terms = [token_terms(e) for e in range(nb)]
    col = lambda name, e, p: terms[e][name][:, p * pw_lanes:(p + 1) * pw_lanes]
    of_pair = lambda xs, e, p: [x for x, it in zip(xs, items) if it[:2] == (e, p)]
    by_head = lambda rows, x0, x1: jnp.where(head_of(rows), x1, x0)

    own = lambda rows, hh: head_of(rows) if hh else jnp.logical_not(head_of(rows))
    aa = [_mm_nt(jnp.where(own(2 * c, hh), jnp.concatenate([col('kt', e, p), col('rt_b', e, p)], axis=0), 0.0),
                 jnp.concatenate([col('nbt', e, p), col('kti', e, p)], axis=0))
          for e, p, hh in items]
    a_ub = [jnp.where(tril_strict, x[:c, :c], 0.0) for x in aa]
    a_uk = [jnp.where(uk_mask, x[:c, :], 0.0).astype(BF16) for x in aa]
    a_r = [jnp.where(ar_mask, x[c:, :], 0.0).astype(BF16) for x in aa]
    tinv = [eye_c + x for x in a_ub]
    pw = [x.astype(BF16) for x in a_ub]
    span = 2
    while span < c:
        pw = [jnp.dot(x, x, preferred_element_type=F32).astype(BF16) for x in pw]
        tinv = [t + jnp.dot(x, t.astype(BF16), preferred_element_type=F32) for x, t in zip(pw, tinv)]
        span *= 2
    av = [jnp.dot(x, jnp.concatenate([zeros_c, col('vb', e, p)], axis=0), preferred_element_type=F32)
          for x, (e, p, hh) in zip(a_uk, items)]
    wu = [_mm(t, jnp.concatenate([col('kt', e, p), x.astype(BF16)], axis=1))
          for t, x, (e, p, hh) in zip(tinv, av, items)]
    zt = [_mm_nt(pick[hh], jnp.concatenate([col('nbh', e, p), col('kh', e, p)], axis=0))
          for e, p, hh in items]
    rhs2 = []
    for e, p in pairs:
        w0, w1 = of_pair(wu, e, p)
        w_pair = by_head(c, w0[:, :pw_lanes], w1[:, :pw_lanes]).astype(BF16)
        u_pair = by_head(c, w0[:, pw_lanes:], w1[:, pw_lanes:]).astype(BF16)
        rhs2.append(jnp.concatenate([jnp.concatenate([w_pair, zeros_c], axis=0),
                                     jnp.concatenate([u_pair, col('vb', e, p)], axis=0)], axis=1))
    res = [_mm(jnp.concatenate([x, z.astype(BF16)], axis=0), rhs2[pairs.index((e, p))])
           for x, z, (e, p, hh) in zip(a_r, zt, items)]
    for i, (e, p) in enumerate(pairs):
        r0, r1 = of_pair(res, e, p)
        top = jnp.where(second_wide(c), r1[:c], r0[:c])
        low = jnp.concatenate([jnp.where(second_wide(n), 0.0, r0[c:]),
                               jnp.where(second_wide(n), r1[c:], 0.0)], axis=0)
        m_bd = jnp.where(eye_pair, col('p_end', e, p), 0.0) + low[:, :pw_lanes]
        ro = _mm(jnp.concatenate([col('rt', e, p) + top[:, :pw_lanes], m_bd], axis=0), h_scr[i])
        o = ro[:c] + top[:, pw_lanes:]
        h_scr[i] = ro[c:] + low[:, pw_lanes:]
        second = head_of(c)
        halves = lambda x: by_head(c, jnp.sum(jnp.where(second, 0.0, x), axis=-1, keepdims=True),
                                   jnp.sum(jnp.where(second, x, 0.0), axis=-1, keepdims=True))
        mean = halves(o) * (1.0 / n)
        var = halves(jnp.square(o - mean)) * (1.0 / n)
        on = (o - mean) * lax.rsqrt(var + GN_EPS)
        bonus = halves(col('bonus_rk', e, p)) * col('v', e, p)
        lanes = slice(p * pw_lanes, (p + 1) * pw_lanes)
        out = (on * lng_ref[:, lanes] + lnb_ref[:, lanes] + bonus) * col('g', e, p)
        og_ref[e, :, lanes] = out[:tc]

    @pl.when(step == pl.num_programs(1) - 1)
    def _():
        for i, (e, p) in enumerate(pairs):
            hbd = h_scr[i]
            sout_ref[e, 2 * p] = _transpose_f32(hbd[:n, :n])
            sout_ref[e, 2 * p + 1] = _transpose_f32(hbd[n:, n:])


def _wkv_scan(seven, rk, lng, lnb, state0):
    b, t, d = seven[0].shape
    heads = d // RW_HEAD
    tc = min(SCAN_CHUNK, t)
    c = max(tc, SCAN_MIN_CHUNK)
    nb = SCAN_SEQS_PER_STEP if b % SCAN_SEQS_PER_STEP == 0 else 1
    assert t % tc == 0 and heads % 2 == 0
    blk = pl.BlockSpec((nb, tc, d), lambda bi, j: (bi, j, 0))
    st = pl.BlockSpec((nb, heads, RW_HEAD, RW_HEAD), lambda bi, j: (bi, 0, 0, 0))
    og, s_out = pl.pallas_call(
        functools.partial(_wkv_chunk_kernel, nb, tc, c, heads),
        grid=(b // nb, t // tc),
        in_specs=[blk] * 7 + [_const_spec((1, d))] * 3 + [st],
        out_specs=[blk, st],
        out_shape=[jax.ShapeDtypeStruct((b, t, d), F32),
                   jax.ShapeDtypeStruct((b, heads, RW_HEAD, RW_HEAD), F32)],
        scratch_shapes=[pltpu.VMEM((nb * (heads // 2), 2 * RW_HEAD, 2 * RW_HEAD), F32)],
        compiler_params=_params(("parallel", "arbitrary")),
    )(*seven, rk.reshape(1, d).astype(F32), lng.reshape(1, d).astype(F32),
      lnb.reshape(1, d).astype(F32), state0)
    return og, s_out


def _wkv_lanes_kernel(steps, r_ref, w_ref, k_ref, v_ref, kk_ref, a_ref, g_ref, rk_ref, lng_ref, lnb_ref,
                      s0_ref, og_ref, sout_ref):
    s = s0_ref[0]
    for t in range(steps):
        kk = kk_ref[t]
        k = k_ref[t]
        v = v_ref[t]
        r = r_ref[t]
        sa = jnp.sum(s * kk[None], axis=1)
        s = s * jnp.exp(w_ref[t])[None] - sa[:, None, :] * (kk * a_ref[t])[None] + v[:, None, :] * k[None]
        o = jnp.sum(s * r[None], axis=1)
        mean = jnp.mean(o, axis=0, keepdims=True)
        var = jnp.mean(jnp.square(o - mean), axis=0, keepdims=True)
        on = (o - mean) * lax.rsqrt(var + GN_EPS)
        bonus = jnp.sum(r * k * rk_ref[...], axis=0, keepdims=True) * v
        og_ref[t] = (on * lng_ref[...] + lnb_ref[...] + bonus) * g_ref[t]
    sout_ref[0] = s


def _wkv_lanes(seven, rk, lng, lnb, state0_t):
    b, t, d = seven[0].shape
    heads = d // RW_HEAD
    tok = [jnp.transpose(x, (1, 2, 0)) for x in seven]
    col = lambda p: jnp.broadcast_to(p.reshape(d, 1).astype(F32), (d, b))
    blk = pl.BlockSpec((t, RW_HEAD, b), lambda h: (0, h, 0))
    par = pl.BlockSpec((RW_HEAD, b), lambda h: (h, 0))
    st = pl.BlockSpec((1, RW_HEAD, RW_HEAD, b), lambda h: (h, 0, 0, 0))
    og, s_out = pl.pallas_call(
        functools.partial(_wkv_lanes_kernel, t),
        grid=(heads,),
        in_specs=[blk] * 7 + [par] * 3 + [st],
        out_specs=[blk, st],
        out_shape=[jax.ShapeDtypeStruct((t, d, b), F32), jax.ShapeDtypeStruct(state0_t.shape, F32)],
        compiler_params=_params(("parallel",)),
    )(*tok, col(rk), col(lng), col(lnb), state0_t)
    return jnp.transpose(og, (2, 0, 1)), s_out


def _post_kernel(mla_heads, final_norm, mix_ref, x_ref, *rest):
    if mla_heads:
        wuv_ref, rest = rest[0], rest[1:]
    wo_ref, nf_ref, up_ref, down_ref = rest[:4]
    rest = rest[4:]
    if final_norm:
        fn_ref, rest = rest[0], rest[1:]
    y_ref = rest[0]
    mix = mix_ref[...]
    if mla_heads:
        rank = mix.shape[1] // mla_heads
        mix = jnp.concatenate(
            [_mm(mix[:, h * rank:(h + 1) * rank], wuv_ref[h]) for h in range(mla_heads)], axis=1)
    x2 = x_ref[...] + _mm(mix, wo_ref[...])
    hn = _rms(x2, nf_ref[...]).astype(BF16)
    acc = x2
    dff = up_ref.shape[1]
    fc = min(FF_CHUNK, dff)
    for cidx in range(dff // fc):
        hid = jnp.dot(hn, up_ref[:, cidx * fc:(cidx + 1) * fc], preferred_element_type=F32)
        hid = jnp.square(jnp.maximum(hid, 0.0)).astype(BF16)
        acc = acc + jnp.dot(hid, down_ref[cidx * fc:(cidx + 1) * fc, :], preferred_element_type=F32)
    if final_norm:
        acc = _rms(acc, fn_ref[...])
    y_ref[...] = acc


def _post(mix, x, wo, nf, up, down, wuv=None, fn=None):
    n, d = x.shape
    tm = min(ROW_TILE, n)
    assert n % tm == 0
    ins = [mix, x]
    specs = [pl.BlockSpec((tm, mix.shape[1]), lambda i: (i, 0)), pl.BlockSpec((tm, d), lambda i: (i, 0))]

    def add_const(arr):
        ins.append(arr)
        specs.append(pl.BlockSpec(arr.shape, lambda i, _nd=arr.ndim: (0,) * _nd,
                                  pipeline_mode=pl.Buffered(1)))

    if wuv is not None:
        add_const(wuv)
    for arr in (wo, nf.reshape(1, d).astype(F32), up, down):
        add_const(arr)
    if fn is not None:
        add_const(fn.reshape(1, d).astype(F32))
    return pl.pallas_call(
        functools.partial(_post_kernel, 0 if wuv is None else wuv.shape[0], fn is not None),
        grid=(n // tm,),
        in_specs=specs,
        out_specs=pl.BlockSpec((tm, d), lambda i: (i, 0)),
        out_shape=jax.ShapeDtypeStruct((n, d), F32),
        compiler_params=_params(("parallel",)),
    )(*ins)


def _mla_pre_kernel(heads, scale, x_ref, cos_ref, sin_ref, kvn_ref, wdkv_ref, latn_ref, wkr_ref, wkrp_ref,
                    nmix_ref, wdq_ref, qn_ref, wuqn_ref, wuqr_ref, wuqrp_ref, wukt_ref,
                    lat_ref, kr_ref, kc_ref, qc_ref):
    x = x_ref[...]
    cos = cos_ref[...]
    sin = sin_ref[...]
    rows = x.shape[0]
    rank = wukt_ref.shape[2]
    kw = kc_ref.shape[1]
    pad = kw - rank - QK_ROPE
    kv_in = _rms(x, kvn_ref[...]).astype(BF16)
    lat = _rms(jnp.dot(kv_in, wdkv_ref[...], preferred_element_type=F32), latn_ref[...])
    kr = (jnp.dot(kv_in, wkr_ref[...], preferred_element_type=F32) * cos[:, :QK_ROPE]
          + jnp.dot(kv_in, wkrp_ref[...], preferred_element_type=F32) * sin[:, :QK_ROPE])
    lat_ref[...] = lat
    kr_ref[...] = kr
    kc_ref[:, :rank] = lat.astype(BF16)
    kc_ref[:, rank:rank + QK_ROPE] = kr.astype(BF16)
    if pad:
        kc_ref[:, rank + QK_ROPE:] = jnp.zeros((rows, pad), BF16)
    xn = _rms(x, nmix_ref[...]).astype(BF16)
    cq = _rms(jnp.dot(xn, wdq_ref[...], preferred_element_type=F32), qn_ref[...]).astype(BF16)
    qn = jnp.dot(cq, wuqn_ref[...], preferred_element_type=F32)
    qr = ((jnp.dot(cq, wuqr_ref[...], preferred_element_type=F32) * cos
           + jnp.dot(cq, wuqrp_ref[...], preferred_element_type=F32) * sin) * scale).astype(BF16)
    for h in range(heads):
        ql = _mm(qn[:, h * QK_NOPE:(h + 1) * QK_NOPE], wukt_ref[h])
        qc_ref[:, h * kw:h * kw + rank] = (ql * scale).astype(BF16)
        qc_ref[:, h * kw + rank:h * kw + rank + QK_ROPE] = qr[:, h * QK_ROPE:(h + 1) * QK_ROPE]
        if pad:
            qc_ref[:, h * kw + rank + QK_ROPE:(h + 1) * kw] = jnp.zeros((rows, pad), BF16)


def _rope_tables(pos, heads):
    half = QK_ROPE // 2
    inv_freq = ROPE_BASE ** (-jnp.arange(half, dtype=F32) / half)
    ang = pos.astype(F32)[:, None] * inv_freq[None, :]
    cos, sin = jnp.cos(ang), jnp.sin(ang)
    cos2 = jnp.concatenate([cos, cos], axis=1)
    sin2 = jnp.concatenate([-sin, sin], axis=1)
    return jnp.tile(cos2, (1, heads)), jnp.tile(sin2, (1, heads))


def _swap_halves(wcols):
    half = QK_ROPE // 2
    return jnp.concatenate([wcols[..., half:], wcols[..., :half]], axis=-1)


def _mla_weights(p, j):
    d, rank = p['w_dkv'].shape
    heads = p['w_uk'].shape[1]
    qrank = p['w_dq'].shape[2]
    wuq = p['w_uq'][j]
    wuq_n = wuq[:, :, :QK_NOPE].reshape(qrank, heads * QK_NOPE)
    wuq_r = wuq[:, :, QK_NOPE:]
    row2 = lambda t: t.reshape(1, -1).astype(F32)
    return dict(
        kvn=row2(p['kv_norm']), wdkv=p['w_dkv'].astype(BF16), latn=row2(p['lat_norm']),
        wkr=p['w_kr'].astype(BF16), wkrp=_swap_halves(p['w_kr']).astype(BF16),
        nmix=row2(p['norm_mix'][p['rw_wr'].shape[0] + j]), wdq=p['w_dq'][j].astype(BF16), qn=row2(p['q_norm'][j]),
        wuqn=wuq_n.astype(BF16), wuqr=wuq_r.reshape(qrank, heads * QK_ROPE).astype(BF16),
        wuqrp=_swap_halves(wuq_r).reshape(qrank, heads * QK_ROPE).astype(BF16),
        wukt=jnp.transpose(p['w_uk'], (1, 2, 0)).astype(BF16),
        wuv=jnp.transpose(p['w_uv'], (1, 0, 2)).astype(BF16),
        heads=heads, rank=rank)


def _key_width(rank):
    return -(-(rank + QK_ROPE) // 128) * 128


def _mla_pre(x, cos, sin, mw):
    n, d = x.shape
    heads, rank = mw['heads'], mw['rank']
    kw = _key_width(rank)
    tm = min(ROW_TILE, n)
    assert n % tm == 0 and cos.shape[0] % tm == 0
    tab_blocks = cos.shape[0] // tm
    row = lambda width: pl.BlockSpec((tm, width), lambda i: (i, 0))
    tab = pl.BlockSpec((tm, heads * QK_ROPE), lambda i: (i % tab_blocks, 0))
    names = ('kvn', 'wdkv', 'latn', 'wkr', 'wkrp', 'nmix', 'wdq', 'qn', 'wuqn', 'wuqr', 'wuqrp', 'wukt')
    consts = [mw[k] for k in names]
    scale = float((QK_NOPE + QK_ROPE) ** -0.5)
    return pl.pallas_call(
        functools.partial(_mla_pre_kernel, heads, scale),
        grid=(n // tm,),
        in_specs=[row(d), tab, tab] + [_const_spec(a.shape) for a in consts],
        out_specs=[row(rank), row(QK_ROPE), row(kw), row(heads * kw)],
        out_shape=[jax.ShapeDtypeStruct((n, rank), F32), jax.ShapeDtypeStruct((n, QK_ROPE), F32),
                   jax.ShapeDtypeStruct((n, kw), BF16), jax.ShapeDtypeStruct((n, heads * kw), BF16)],
        compiler_params=_params(("parallel",)),
    )(x, cos, sin, *consts)


def _softmax_update(s, vals, m_scr, l_scr, acc_scr):
    lanes = m_scr.shape[1]
    width = s.shape[1]
    m_prev = m_scr[...]
    m_new = jnp.maximum(m_prev, jnp.max(s, axis=-1, keepdims=True))
    alpha = jnp.exp(m_prev - m_new)
    m_wide = m_new[:, :width] if width <= lanes else jnp.tile(m_new, (1, width // lanes))
    p = jnp.exp(s - m_wide)
    l_scr[...] = alpha * l_scr[...] + jnp.sum(p, axis=-1, keepdims=True)
    acc_scr[...] = (jnp.tile(alpha, (1, acc_scr.shape[1] // lanes)) * acc_scr[...]
                    + jnp.dot(p.astype(BF16), vals, preferred_element_type=F32))
    m_scr[...] = m_new


def _softmax_init(m_scr, l_scr, acc_scr):
    m_scr[...] = jnp.full(m_scr.shape, MASK_VALUE, F32)
    l_scr[...] = jnp.zeros(l_scr.shape, F32)
    acc_scr[...] = jnp.zeros(acc_scr.shape, F32)


def _softmax_result(l_scr, acc_scr):
    return acc_scr[...] / jnp.tile(l_scr[...], (1, acc_scr.shape[1] // l_scr.shape[1]))


def _attn_prompt_kernel(heads, rank, tk, qc_ref, kc_ref, pkc_ref, o_ref, m_scr, l_scr, acc_scr):
    i = pl.program_id(1)
    tq = qc_ref.shape[0]
    kw = kc_ref.shape[2]
    group = heads // ATTN_ROW_PARTS
    rows = group * tq
    qcs = [jnp.concatenate([qc_ref[:, h * kw:(h + 1) * kw] for h in range(g * group, (g + 1) * group)], axis=0)
           for g in range(ATTN_ROW_PARTS)]
    stats = [(m_scr.at[pl.ds(g * rows, rows)], l_scr.at[pl.ds(g * rows, rows)], acc_scr.at[pl.ds(g * rows, rows)])
             for g in range(ATTN_ROW_PARTS)]
    _softmax_init(m_scr, l_scr, acc_scr)
    pkc = pkc_ref[...]
    for qc, st in zip(qcs, stats):
        _softmax_update(_mm_nt(qc, pkc), pkc[:, :rank], *st)
    n_full = (i * tq) // tk

    def body(jb, carry):
        kc = kc_ref[0, pl.ds(pl.multiple_of(jb * tk, tk), tk), :]
        for qc, st in zip(qcs, stats):
            _softmax_update(_mm_nt(qc, kc), kc[:, :rank], *st)
        return carry

    lax.fori_loop(0, n_full, body, 0)
    kc = kc_ref[0, pl.ds(pl.multiple_of(n_full * tk, tk), tk), :]
    q_pos = i * tq + lax.broadcasted_iota(jnp.int32, (rows, tk), 0) % tq
    k_pos = n_full * tk + lax.broadcasted_iota(jnp.int32, (rows, tk), 1)
    for qc, st in zip(qcs, stats):
        _softmax_update(jnp.where(k_pos <= q_pos, _mm_nt(qc, kc), MASK_VALUE), kc[:, :rank], *st)
    o = _softmax_result(l_scr, acc_scr).astype(BF16)
    for h in range(heads):
        o_ref[:, h * rank:(h + 1) * rank] = o[h * tq:(h + 1) * tq, :]


def _attn_prompt(qc, kc, pkc, batch, heads, rank):
    n = qc.shape[0]
    t = n // batch
    kw = kc.shape[1]
    tq = min(ATTN_TILE, t)
    tk = min(ATTN_KEY_TILE, t)
    assert t % tk == 0 and tk % tq == 0
    nq = t // tq
    return pl.pallas_call(
        functools.partial(_attn_prompt_kernel, heads, rank, tk),
        grid=(batch, nq),
        in_specs=[pl.BlockSpec((tq, heads * kw), lambda b, i: (b * nq + i, 0)),
                  pl.BlockSpec((1, t, kw), lambda b, i: (b, 0, 0)),
                  _const_spec(pkc.shape)],
        out_specs=pl.BlockSpec((tq, heads * rank), lambda b, i: (b * nq + i, 0)),
        out_shape=jax.ShapeDtypeStruct((n, heads * rank), BF16),
        scratch_shapes=[pltpu.VMEM((heads * tq, 128), F32), pltpu.VMEM((heads * tq, 128), F32),
                        pltpu.VMEM((heads * tq, rank), F32)],
        compiler_params=_params(("parallel", "arbitrary")),
    )(qc, kc.reshape(batch, t, kw), pkc)


def _attn_sample_kernel(chunk_pages, seq, rank, pt_ref, qc_ref, nkc_ref, lat_hbm, krt_hbm, o_ref,
                        lat_buf, krt_buf, sems, m_scr, l_scr, acc_scr):
    b = pl.program_id(0)
    qc = qc_ref[0]

    def page_copies(seq_idx, chunk, slot, p):
        page = pt_ref[seq_idx, chunk * chunk_pages + p]
        rows = pl.ds(pl.multiple_of(p * PAGE_SIZE, PAGE_SIZE), PAGE_SIZE)
        return (pltpu.make_async_copy(lat_hbm.at[page], lat_buf.at[slot, rows, :], sems.at[slot, 0]),
                pltpu.make_async_copy(krt_hbm.at[page], krt_buf.at[slot, :, rows], sems.at[slot, 1]))

    def start_chunk(seq_idx, chunk, slot):
        def body(p, carry):
            for cp in page_copies(seq_idx, chunk, slot, p):
                cp.start()
            return carry
        lax.fori_loop(0, chunk_pages, body, 0, unroll=min(8, chunk_pages))

    def wait_chunk(seq_idx, chunk, slot):
        def body(p, carry):
            for cp in page_copies(seq_idx, chunk, slot, p):
                cp.wait()
            return carry
        lax.fori_loop(0, chunk_pages, body, 0, unroll=min(8, chunk_pages))

    @pl.when(b == 0)
    def _():
        start_chunk(0, 0, 0)

    _softmax_init(m_scr, l_scr, acc_scr)
    for chunk in range(SAMPLE_CHUNKS):
        slot = chunk % 2
        wait_chunk(b, chunk, slot)
        if chunk + 1 < SAMPLE_CHUNKS:
            start_chunk(b, chunk + 1, 1 - slot)
        else:
            @pl.when(b + 1 < pl.num_programs(0))
            def _():
                start_chunk(b + 1, 0, 1 - slot)
        kl = lat_buf[slot].astype(BF16)
        krt = krt_buf[slot].astype(BF16)
        s = _mm_nt(qc[:, :rank], kl) + jnp.dot(qc[:, rank:rank + QK_ROPE], krt, preferred_element_type=F32)
        _softmax_update(s, kl, m_scr, l_scr, acc_scr)

    nkc = nkc_ref[0]
    s = _mm_nt(qc, nkc)
    q_t = lax.broadcasted_iota(jnp.int32, s.shape, 0) % seq
    k_t = lax.broadcasted_iota(jnp.int32, s.shape, 1)
    s = jnp.where(k_t <= q_t, s, MASK_VALUE)
    _softmax_update(s, nkc[:, :rank], m_scr, l_scr, acc_scr)
    o_ref[0] = _softmax_result(l_scr, acc_scr).astype(BF16)


def _attn_sample(qc, nkc, seq, rank, cache_latent, cache_krope_t, page_table):
    b, rows, kw = qc.shape
    total_pages = page_table.shape[1]
    assert SAMPLE_CHUNKS % 2 == 0 and total_pages % SAMPLE_CHUNKS == 0
    chunk_pages = total_pages // SAMPLE_CHUNKS
    chunk_keys = chunk_pages * PAGE_SIZE
    per_b = lambda arr: pl.BlockSpec((1,) + arr.shape[1:], lambda bi, pt: (bi, 0, 0))
    grid_spec = pltpu.PrefetchScalarGridSpec(
        num_scalar_prefetch=1,
        grid=(b,),
        in_specs=[per_b(qc), per_b(nkc), pl.BlockSpec(memory_space=pl.ANY), pl.BlockSpec(memory_space=pl.ANY)],
        out_specs=pl.BlockSpec((1, rows, rank), lambda bi, pt: (bi, 0, 0)),
        scratch_shapes=[pltpu.VMEM((2, chunk_keys, rank), F32), pltpu.VMEM((2, QK_ROPE, chunk_keys), F32),
                        pltpu.SemaphoreType.DMA((2, 2)),
                        pltpu.VMEM((rows, 128), F32), pltpu.VMEM((rows, 128), F32), pltpu.VMEM((rows, rank), F32)],
    )
    return pl.pallas_call(
        functools.partial(_attn_sample_kernel, chunk_pages, seq, rank),
        grid_spec=grid_spec,
        out_shape=jax.ShapeDtypeStruct((b, rows, rank), BF16),
        compiler_params=_params(("arbitrary",)),
    )(page_table, qc, nkc, cache_latent, cache_krope_t)


def _layer0(x, shift0, wkv0, p, flat):
    b, t, d = x.shape
    weights = _rwkv_pre_weights(p, 0)
    pre = _rwkv_pre_flat if flat else _rwkv_pre_seq
    seven, shift_new = pre(x, shift0, p['norm_mix'][0], weights)
    scan = _wkv_lanes if flat else _wkv_scan
    og, wkv_new = scan(seven, p['rw_rk'][0], p['rw_lnx_g'][0], p['rw_lnx_b'][0], wkv0)
    x1 = _post(og.reshape(b * t, d), x.reshape(b * t, d), p['rw_wo'][0].astype(BF16), p['norm_ffn'][0],
               p['ffn_up'][0].astype(BF16), p['ffn_down'][0].astype(BF16))
    return x1, wkv_new, shift_new


def _layer1_post(olat, x1, p, mw):
    return _post(olat, x1, p['w_o_mla'][0].astype(BF16), p['norm_ffn'][1], p['ffn_up'][1].astype(BF16),
                 p['ffn_down'][1].astype(BF16), wuv=mw['wuv'], fn=p['norm_final'])


def kernel(x_prompt, x_sample, state_wkv, state_shift, cache_latent, cache_krope, page_table,
           meta_tokens, rw_mu, rw_wr, rw_wk, rw_wv, rw_wo, rw_w0, rw_w1, rw_w2, rw_a0, rw_a1, rw_a2,
           rw_g1, rw_g2, rw_kk, rw_ka, rw_rk, rw_lnx_g, rw_lnx_b, norm_mix, norm_ffn, ffn_up, ffn_down,
           kv_norm, w_dkv, lat_norm, w_kr, w_uk, w_uv, w_dq, q_norm, w_uq, w_o_mla, norm_final):
    p = dict(rw_mu=rw_mu, rw_wr=rw_wr, rw_wk=rw_wk, rw_wv=rw_wv, rw_wo=rw_wo, rw_w0=rw_w0, rw_w1=rw_w1,
             rw_w2=rw_w2, rw_a0=rw_a0, rw_a1=rw_a1, rw_a2=rw_a2, rw_g1=rw_g1, rw_g2=rw_g2, rw_kk=rw_kk,
             rw_ka=rw_ka, rw_rk=rw_rk, rw_lnx_g=rw_lnx_g, rw_lnx_b=rw_lnx_b, norm_mix=norm_mix,
             norm_ffn=norm_ffn, ffn_up=ffn_up, ffn_down=ffn_down, kv_norm=kv_norm, w_dkv=w_dkv,
             lat_norm=lat_norm, w_kr=w_kr, w_uk=w_uk, w_uv=w_uv, w_dq=w_dq, q_norm=q_norm, w_uq=w_uq,
             w_o_mla=w_o_mla, norm_final=norm_final)
    assert rw_wr.shape[0] == 1 and w_dq.shape[0] == 1, "one RWKV layer followed by one MLA layer"
    bsz, seq, d = x_prompt.shape
    dbsz, dseq, _ = x_sample.shape
    rw_heads = d // RW_HEAD
    mw = _mla_weights(p, 0)
    heads, rank = mw['heads'], mw['rank']
    n_meta = meta_tokens.shape[0]
    past_len = page_table.shape[1] * PAGE_SIZE

    xm1, wkv_m, shift_m = _layer0(meta_tokens[None].astype(F32), jnp.zeros((1, d), F32),
                                  jnp.zeros((1, rw_heads, RW_HEAD, RW_HEAD), F32), p, flat=False)
    cos_m, sin_m = _rope_tables(jnp.arange(n_meta, dtype=jnp.int32), heads)
    lat_m, kr_m, kc_m, _ = _mla_pre(xm1, cos_m, sin_m, mw)

    xp1, wkv_p, shift_p = _layer0(x_prompt, jnp.broadcast_to(shift_m, (bsz, d)),
                                  jnp.broadcast_to(wkv_m, (bsz,) + wkv_m.shape[1:]), p, flat=False)
    cos_p, sin_p = _rope_tables(n_meta + jnp.arange(seq, dtype=jnp.int32), heads)
    lat_p, kr_p, kc_p, qc_p = _mla_pre(xp1, cos_p, sin_p, mw)
    olat_p = _attn_prompt(qc_p, kc_p, kc_m, bsz, heads, rank)
    y_prompt = _layer1_post(olat_p, xp1, p, mw).reshape(bsz, seq, d)
    latent_prompt = jnp.concatenate(
        [jnp.broadcast_to(lat_m[None], (bsz, n_meta, rank)), lat_p.reshape(bsz, seq, rank)], axis=1)
    krope_prompt = jnp.concatenate(
        [jnp.broadcast_to(kr_m[None], (bsz, n_meta, QK_ROPE)), kr_p.reshape(bsz, seq, QK_ROPE)], axis=1)

    xs1, wkv_s, shift_s = _layer0(x_sample, state_shift[0], jnp.transpose(state_wkv[0], (1, 2, 3, 0)), p, flat=True)
    wkv_s = jnp.transpose(wkv_s, (3, 0, 1, 2))
    cos_s, sin_s = _rope_tables(past_len + jnp.arange(dseq, dtype=jnp.int32), heads)
    cos_s, sin_s = jnp.tile(cos_s, (dbsz, 1)), jnp.tile(sin_s, (dbsz, 1))
    lat_s, kr_s, kc_s, qc_s = _mla_pre(xs1, cos_s, sin_s, mw)
    kw = kc_s.shape[1]
    qc_s = jnp.transpose(qc_s.reshape(dbsz, dseq, heads, kw), (0, 2, 1, 3)).reshape(dbsz, heads * dseq, kw)
    new_rows = -(-dseq // 16) * 16
    nkc_s = jnp.pad(kc_s.reshape(dbsz, dseq, kw), ((0, 0), (0, new_rows - dseq), (0, 0)))
    olat_s = _attn_sample(qc_s, nkc_s, dseq, rank, cache_latent, jnp.swapaxes(cache_krope, 1, 2), page_table)
    olat_s = jnp.transpose(olat_s.reshape(dbsz, heads, dseq, rank), (0, 2, 1, 3)).reshape(dbsz * dseq, heads * rank)
    y_sample = _layer1_post(olat_s, xs1, p, mw).reshape(dbsz, dseq, d)

    return (y_prompt, y_sample, wkv_p[None], shift_p[None], latent_prompt, krope_prompt,
            wkv_s[None], shift_s[None], lat_s.reshape(dbsz, dseq, rank), kr_s.reshape(dbsz, dseq, QK_ROPE))
```

```python
import functools
import math

import jax
import jax.numpy as jnp
from jax import lax
from jax.experimental import pallas as pl
from jax.experimental.pallas import tpu as pltpu

F32 = jnp.float32
BF16 = jnp.bfloat16

RW_HEAD = 64
QK_NOPE = 128
QK_ROPE = 64
V_HEAD = 128
N_META = 16
PAGE_SIZE = 128
NORM_EPS = 1e-6
GN_EPS = 64e-5
ROPE_BASE = 10000.0
DECAY_SCALE = math.exp(-0.5)
MASK_VALUE = -1e30

V7X_VMEM_LIMIT_BYTES = 56 * 1024 * 1024
SCAN_CHUNK = 64
SCAN_MIN_CHUNK = 16
SCAN_SEQS_PER_STEP = 2
ROW_TILE = 256
FF_CHUNK = 1024
ATTN_TILE = 256
ATTN_KEY_TILE = 512
ATTN_ROW_PARTS = 4
SAMPLE_CHUNKS = 2
SAMPLE_SLOTS = 3


def _mm(a, b):
    return jnp.dot(a.astype(BF16), b.astype(BF16), preferred_element_type=F32)


def _mm_nt(a, b):
    return lax.dot_general(a.astype(BF16), b.astype(BF16), (((1,), (1,)), ((), ())),
                           preferred_element_type=F32)


def _split3(x):
    h = x.astype(BF16)
    r1 = x - h.astype(F32)
    m = r1.astype(BF16)
    lo = (r1 - m.astype(F32)).astype(BF16)
    return h, m, lo


def _mm_exact_lhs(sel, x):
    h, m, lo = _split3(x)
    sel = sel.astype(BF16)
    dot = lambda t: jnp.dot(sel, t, preferred_element_type=F32)
    return dot(h) + (dot(m) + dot(lo))


def _transpose_f32(x):
    n = x.shape[1]
    eye = (lax.broadcasted_iota(jnp.int32, (n, n), 0) == lax.broadcasted_iota(jnp.int32, (n, n), 1))
    eye = eye.astype(BF16)
    h, m, lo = _split3(x)
    dot = lambda t: lax.dot_general(eye, t, (((1,), (1,)), ((), ())), preferred_element_type=F32)
    return dot(h) + (dot(m) + dot(lo))


def _rms(x, g):
    return x * lax.rsqrt(jnp.mean(x * x, axis=-1, keepdims=True) + NORM_EPS) * g


def _sigmoid(z):
    return 1.0 / (1.0 + jnp.exp(-z))


def _const_spec(shape):
    nd = len(shape)
    return pl.BlockSpec(shape, lambda *_: (0,) * nd)


def _params(sem):
    return pltpu.CompilerParams(dimension_semantics=sem, vmem_limit_bytes=V7X_VMEM_LIMIT_BYTES)


def _rwkv_pre_math(xn, prev, w, outs):
    (mu, wr, wk, wv, w0, w1, w2, a0, a1, a2, g1, g2, kkp, kap, hsum, hbc) = w
    r_ref, w_ref, k_ref, v_ref, kk_ref, a_ref, g_ref = outs
    xx = prev - xn
    mix = lambda m: (xn + xx * mu[m:m + 1, :]).astype(BF16)
    r = _mm(mix(0), wr[...])
    z = w0[...] + _mm(jnp.tanh(_mm(mix(1), w1[...])), w2[...])
    wdec = -DECAY_SCALE * _sigmoid(z)
    k = _mm(mix(2), wk[...])
    v = _mm(mix(3), wv[...])
    a = _sigmoid(a0[...] + _mm(_mm(mix(4), a1[...]), a2[...]))
    g = _mm(_sigmoid(_mm(mix(5), g1[...])), g2[...])
    kkr = k * kkp[...]
    sq = kkr * kkr
    sq_h = sq.astype(BF16)
    sq_l = (sq - sq_h.astype(F32)).astype(BF16)
    ssq = (jnp.dot(sq_h, hsum[...], preferred_element_type=F32)
           + jnp.dot(sq_l, hsum[...], preferred_element_type=F32))
    inv = jnp.minimum(lax.rsqrt(ssq), 1e12)
    inv_h = inv.astype(BF16)
    inv_l = (inv - inv_h.astype(F32)).astype(BF16)
    inv_b = (jnp.dot(inv_h, hbc[...], preferred_element_type=F32)
             + jnp.dot(inv_l, hbc[...], preferred_element_type=F32))
    r_ref[...] = r.reshape(r_ref.shape)
    w_ref[...] = wdec.reshape(w_ref.shape)
    k_ref[...] = (k * (1.0 + (a - 1.0) * kap[...])).reshape(k_ref.shape)
    v_ref[...] = v.reshape(v_ref.shape)
    kk_ref[...] = (kkr * inv_b).reshape(kk_ref.shape)
    a_ref[...] = a.reshape(a_ref.shape)
    g_ref[...] = g.reshape(g_ref.shape)


def _rwkv_pre_seq_kernel(x_ref, halo_ref, s0_ref, gmix_ref, *rest):
    w, outs, tail_ref = rest[:16], rest[16:23], rest[23]
    j = pl.program_id(1)
    x = x_ref[0]
    tm = x.shape[0]
    gm = gmix_ref[...]
    xn = _rms(x, gm)
    row = lax.broadcasted_iota(jnp.int32, (tm, 1), 0)
    prev_raw = jnp.where(row == 0, halo_ref[0][7:8, :], pltpu.roll(x, 1, 0))
    prev = jnp.where((row == 0) & (j == 0), s0_ref[0], _rms(prev_raw, gm))
    tail_ref[0] = xn[tm - 8:, :]
    _rwkv_pre_math(xn, prev, w, outs)


def _rwkv_pre_flat_kernel(seq_len, x_ref, xs_ref, s0x_ref, gmix_ref, *rest):
    w, outs, xn_ref = rest[:16], rest[16:23], rest[23]
    x = x_ref[...]
    tm = x.shape[0]
    gm = gmix_ref[...]
    xn = _rms(x, gm)
    row = lax.broadcasted_iota(jnp.int32, (tm, 1), 0)
    prev = jnp.where(row % seq_len == 0, s0x_ref[...], _rms(xs_ref[...], gm))
    xn_ref[...] = xn
    _rwkv_pre_math(xn, prev, w, outs)


def _rwkv_pre_weights(p, i):
    d = p['rw_wr'].shape[-1]
    heads = d // RW_HEAD
    head_of = jnp.arange(d, dtype=jnp.int32) // RW_HEAD
    hsum = (head_of[:, None] == jnp.arange(heads, dtype=jnp.int32)[None, :]).astype(BF16)
    row2 = lambda t: t.reshape(1, -1).astype(F32)
    return (p['rw_mu'][i].astype(F32), p['rw_wr'][i].astype(BF16), p['rw_wk'][i].astype(BF16),
            p['rw_wv'][i].astype(BF16), row2(p['rw_w0'][i]), p['rw_w1'][i].astype(BF16),
            p['rw_w2'][i].astype(BF16), row2(p['rw_a0'][i]), p['rw_a1'][i].astype(BF16),
            p['rw_a2'][i].astype(BF16), p['rw_g1'][i].astype(BF16), p['rw_g2'][i].astype(BF16),
            row2(p['rw_kk'][i]), row2(p['rw_ka'][i]), hsum, hsum.T)


def _rwkv_pre_seq(x, shift0, gmix, weights):
    b, t, d = x.shape
    tm = min(ROW_TILE, t)
    assert t % tm == 0 and tm % 8 == 0
    blk = pl.BlockSpec((1, tm, d), lambda bi, j: (bi, j, 0))
    halo = pl.BlockSpec((1, 8, d), lambda bi, j: (bi, jnp.maximum(j * (tm // 8) - 1, 0), 0))
    s0 = pl.BlockSpec((1, 1, d), lambda bi, j: (bi, 0, 0))
    tail = pl.BlockSpec((1, 8, d), lambda bi, j: (bi, 0, 0))
    out = pl.pallas_call(
        _rwkv_pre_seq_kernel,
        grid=(b, t // tm),
        in_specs=[blk, halo, s0, _const_spec((1, d))] + [_const_spec(w.shape) for w in weights],
        out_specs=[blk] * 7 + [tail],
        out_shape=[jax.ShapeDtypeStruct((b, t, d), F32)] * 7 + [jax.ShapeDtypeStruct((b, 8, d), F32)],
        compiler_params=_params(("parallel", "arbitrary")),
    )(x, x, shift0.reshape(b, 1, d), gmix.reshape(1, d), *weights)
    return out[:7], out[7][:, 7, :]


def _rwkv_pre_flat(x, shift0, gmix, weights):
    b, t, d = x.shape
    n = b * t
    tm = min(ROW_TILE, n)
    assert n % tm == 0 and tm % t == 0
    xf = x.reshape(n, d)
    xs = jnp.concatenate([jnp.zeros((1, d), F32), xf[:-1]], axis=0)
    s0x = jnp.concatenate([shift0[:, None, :], jnp.zeros((b, t - 1, d), F32)], axis=1).reshape(n, d)
    blk = pl.BlockSpec((tm, d), lambda i: (i, 0))
    out = pl.pallas_call(
        functools.partial(_rwkv_pre_flat_kernel, t),
        grid=(n // tm,),
        in_specs=[blk, blk, blk, _const_spec((1, d))] + [_const_spec(w.shape) for w in weights],
        out_specs=[blk] * 8,
        out_shape=[jax.ShapeDtypeStruct((n, d), F32)] * 8,
        compiler_params=_params(("parallel",)),
    )(xf, xs, s0x, gmix.reshape(1, d), *weights)
    seven = [o.reshape(b, t, d) for o in out[:7]]
    return seven, out[7].reshape(b, t, d)[:, t - 1, :]


def _wkv_chunk_kernel(nb, tc, c, heads, r_ref, w_ref, k_ref, v_ref, kk_ref, a_ref, g_ref,
                      rk_ref, lng_ref, lnb_ref, s0_ref, og_ref, sout_ref, h_scr):
    step = pl.program_id(1)
    n = RW_HEAD
    pw_lanes = 2 * n
    npairs = heads // 2
    pairs = [(e, p) for e in range(nb) for p in range(npairs)]
    items = [(e, p, hh) for e, p in pairs for hh in range(2)]
    zeros_nn = jnp.zeros((n, n), F32)

    @pl.when(step == 0)
    def _():
        for i, (e, p) in enumerate(pairs):
            t0 = _transpose_f32(s0_ref[e, 2 * p])
            t1 = _transpose_f32(s0_ref[e, 2 * p + 1])
            h_scr[i] = jnp.concatenate([jnp.concatenate([t0, zeros_nn], axis=1),
                                        jnp.concatenate([zeros_nn, t1], axis=1)], axis=0)

    iota = lambda shape, dim: lax.broadcasted_iota(jnp.int32, shape, dim)
    tril_incl = iota((c, c), 0) >= iota((c, c), 1)
    tril_strict = iota((c, c), 0) > iota((c, c), 1)
    eye_c = (iota((c, c), 0) == iota((c, c), 1)).astype(F32)
    uk_mask = (iota((c, 2 * c), 1) >= c) & (iota((c, 2 * c), 0) > iota((c, 2 * c), 1) - c)
    ar_mask = iota((c, 2 * c), 0) >= iota((c, 2 * c), 1) % c
    head_of = lambda rows: iota((rows, pw_lanes), 1) >= n
    second_wide = lambda rows: iota((rows, 2 * pw_lanes), 1) % pw_lanes >= n
    eye_pair = iota((pw_lanes, pw_lanes), 0) == iota((pw_lanes, pw_lanes), 1)
    pick = [iota((n, pw_lanes), 1) == iota((n, pw_lanes), 0) + n * hh for hh in range(2)]
    zeros_c = jnp.zeros((c, pw_lanes), BF16)

    def token_terms(e):
        def load(ref):
            x = ref[e]
            if tc < c:
                x = jnp.concatenate([x, jnp.zeros((c - tc, x.shape[1]), F32)], axis=0)
            return x

        r, w, k, v, kk, a, g = (load(t) for t in (r_ref, w_ref, k_ref, v_ref, kk_ref, a_ref, g_ref))
        lw = _mm_exact_lhs(tril_incl, w)
        lw_end = lw[c - 1:c, :]
        p_inv = jnp.exp(-lw)
        p_tail = jnp.exp(lw_end - lw)
        kb = kk * a
        rt = r * jnp.exp(lw)
        return dict(
            v=v, g=g, rt=rt, p_end=jnp.exp(lw_end), bonus_rk=r * k * rk_ref[...],
            kt=(kk * jnp.exp(lw - w)).astype(BF16), rt_b=rt.astype(BF16), nbt=(-(kb * p_inv)).astype(BF16),
            kti=(k * p_inv).astype(BF16), nbh=(-(kb * p_tail)).astype(BF16), kh=(k * p_tail).astype(BF16),
            vb=v.astype(BF16))

    terms = [token_terms(e) for e in range(nb)]
    col = lambda name, e, p: terms[e][name][:, p * pw_lanes:(p + 1) * pw_lanes]
    of_pair = lambda xs, e, p: [x for x, it in zip(xs, items) if it[:2] == (e, p)]
    by_head = lambda rows, x0, x1: jnp.where(head_of(rows), x1, x0)

    own = lambda rows, hh: head_of(rows) if hh else jnp.logical_not(head_of(rows))
    aa = [_mm_nt(jnp.where(own(2 * c, hh), jnp.concatenate([col('kt', e, p), col('rt_b', e, p)], axis=0), 0.0),
                 jnp.concatenate([col('nbt', e, p), col('kti', e, p)], axis=0))
          for e, p, hh in items]
    a_ub = [jnp.where(tril_strict, x[:c, :c], 0.0) for x in aa]
    a_uk = [jnp.where(uk_mask, x[:c, :], 0.0).astype(BF16) for x in aa]
    a_r = [jnp.where(ar_mask, x[c:, :], 0.0).astype(BF16) for x in aa]
    tinv = [eye_c + x for x in a_ub]
    pw = [x.astype(BF16) for x in a_ub]
    span = 2
    while span < c:
        pw = [jnp.dot(x, x, preferred_element_type=F32).astype(BF16) for x in pw]
        tinv = [t + jnp.dot(x, t.astype(BF16), preferred_element_type=F32) for x, t in zip(pw, tinv)]
        span *= 2
    av = [jnp.dot(x, jnp.concatenate([zeros_c, col('vb', e, p)], axis=0), preferred_element_type=F32)
          for x, (e, p, hh) in zip(a_uk, items)]
    wu = [_mm(t, jnp.concatenate([col('kt', e, p), x.astype(BF16)], axis=1))
          for t, x, (e, p, hh) in zip(tinv, av, items)]
    zt = [_mm_nt(pick[hh], jnp.concatenate([col('nbh', e, p), col('kh', e, p)], axis=0))
          for e, p, hh in items]
    rhs2 = []
    for e, p in pairs:
        w0, w1 = of_pair(wu, e, p)
        w_pair = by_head(c, w0[:, :pw_lanes], w1[:, :pw_lanes]).astype(BF16)
        u_pair = by_head(c, w0[:, pw_lanes:], w1[:, pw_lanes:]).astype(BF16)
        rhs2.append(jnp.concatenate([jnp.concatenate([w_pair, zeros_c], axis=0),
                                     jnp.concatenate([u_pair, col('vb', e, p)], axis=0)], axis=1))
    res = [_mm(jnp.concatenate([x, z.astype(BF16)], axis=0), rhs2[pairs.index((e, p))])
           for x, z, (e, p, hh) in zip(a_r, zt, items)]
    for i, (e, p) in enumerate(pairs):
        r0, r1 = of_pair(res, e, p)
        top = jnp.where(second_wide(c), r1[:c], r0[:c])
        low = jnp.concatenate([jnp.where(second_wide(n), 0.0, r0[c:]),
                               jnp.where(second_wide(n), r1[c:], 0.0)], axis=0)
        m_bd = jnp.where(eye_pair, col('p_end', e, p), 0.0) + low[:, :pw_lanes]
        ro = _mm(jnp.concatenate([col('rt', e, p) + top[:, :pw_lanes], m_bd], axis=0), h_scr[i])
        o = ro[:c] + top[:, pw_lanes:]
        h_scr[i] = ro[c:] + low[:, pw_lanes:]
        second = head_of(c)
        halves = lambda x: by_head(c, jnp.sum(jnp.where(second, 0.0, x), axis=-1, keepdims=True),
                                   jnp.sum(jnp.where(second, x, 0.0), axis=-1, keepdims=True))
        mean = halves(o) * (1.0 / n)
        var = halves(jnp.square(o - mean)) * (1.0 / n)
        on = (o - mean) * lax.rsqrt(var + GN_EPS)
        bonus = halves(col('bonus_rk', e, p)) * col('v', e, p)
        lanes = slice(p * pw_lanes, (p + 1) * pw_lanes)
        out = (on * lng_ref[:, lanes] + lnb_ref[:, lanes] + bonus) * col('g', e, p)
        og_ref[e, :, lanes] = out[:tc]

    @pl.when(step == pl.num_programs(1) - 1)
    def _():
        for i, (e, p) in enumerate(pairs):
            hbd = h_scr[i]
            sout_ref[e, 2 * p] = _transpose_f32(hbd[:n, :n])
            sout_ref[e, 2 * p + 1] = _transpose_f32(hbd[n:, n:])


def _wkv_scan(seven, rk, lng, lnb, state0):
    b, t, d = seven[0].shape
    heads = d // RW_HEAD
    tc = min(SCAN_CHUNK, t)
    c = max(tc, SCAN_MIN_CHUNK)
    nb = SCAN_SEQS_PER_STEP if b % SCAN_SEQS_PER_STEP == 0 else 1
    assert t % tc == 0 and heads % 2 == 0
    blk = pl.BlockSpec((nb, tc, d), lambda bi, j: (bi, j, 0))
    st = pl.BlockSpec((nb, heads, RW_HEAD, RW_HEAD), lambda bi, j: (bi, 0, 0, 0))
    og, s_out = pl.pallas_call(
        functools.partial(_wkv_chunk_kernel, nb, tc, c, heads),
        grid=(b // nb, t // tc),
        in_specs=[blk] * 7 + [_const_spec((1, d))] * 3 + [st],
        out_specs=[blk, st],
        out_shape=[jax.ShapeDtypeStruct((b, t, d), F32),
                   jax.ShapeDtypeStruct((b, heads, RW_HEAD, RW_HEAD), F32)],
        scratch_shapes=[pltpu.VMEM((nb * (heads // 2), 2 * RW_HEAD, 2 * RW_HEAD), F32)],
        compiler_params=_params(("parallel", "arbitrary")),
    )(*seven, rk.reshape(1, d).astype(F32), lng.reshape(1, d).astype(F32),
      lnb.reshape(1, d).astype(F32), state0)
    return og, s_out


def _wkv_lanes_kernel(steps, r_ref, w_ref, k_ref, v_ref, kk_ref, a_ref, g_ref, rk_ref, lng_ref, lnb_ref,
                      s0_ref, og_ref, sout_ref):
    s = s0_ref[0]
    for t in range(steps):
        kk = kk_ref[t]
        k = k_ref[t]
        v = v_ref[t]
        r = r_ref[t]
        sa = jnp.sum(s * kk[None], axis=1)
        s = s * jnp.exp(w_ref[t])[None] - sa[:, None, :] * (kk * a_ref[t])[None] + v[:, None, :] * k[None]
        o = jnp.sum(s * r[None], axis=1)
        mean = jnp.mean(o, axis=0, keepdims=True)
        var = jnp.mean(jnp.square(o - mean), axis=0, keepdims=True)
        on = (o - mean) * lax.rsqrt(var + GN_EPS)
        bonus = jnp.sum(r * k * rk_ref[...], axis=0, keepdims=True) * v
        og_ref[t] = (on * lng_ref[...] + lnb_ref[...] + bonus) * g_ref[t]
    sout_ref[0] = s


def _wkv_lanes(seven, rk, lng, lnb, state0_t):
    b, t, d = seven[0].shape
    heads = d // RW_HEAD
    tok = [jnp.transpose(x, (1, 2, 0)) for x in seven]
    col = lambda p: jnp.broadcast_to(p.reshape(d, 1).astype(F32), (d, b))
    blk = pl.BlockSpec((t, RW_HEAD, b), lambda h: (0, h, 0))
    par = pl.BlockSpec((RW_HEAD, b), lambda h: (h, 0))
    st = pl.BlockSpec((1, RW_HEAD, RW_HEAD, b), lambda h: (h, 0, 0, 0))
    og, s_out = pl.pallas_call(
        functools.partial(_wkv_lanes_kernel, t),
        grid=(heads,),
        in_specs=[blk] * 7 + [par] * 3 + [st],
        out_specs=[blk, st],
        out_shape=[jax.ShapeDtypeStruct((t, d, b), F32), jax.ShapeDtypeStruct(state0_t.shape, F32)],
        compiler_params=_params(("parallel",)),
    )(*tok, col(rk), col(lng), col(lnb), state0_t)
    return jnp.transpose(og, (2, 0, 1)), s_out


def _post_kernel(mla_heads, final_norm, mix_ref, x_ref, *rest):
    if mla_heads:
        wuv_ref, rest = rest[0], rest[1:]
    wo_ref, nf_ref, up_ref, down_ref = rest[:4]
    rest = rest[4:]
    if final_norm:
        fn_ref, rest = rest[0], rest[1:]
    y_ref = rest[0]
    mix = mix_ref[...]
    if mla_heads:
        rank = mix.shape[1] // mla_heads
        mix = jnp.concatenate(
            [_mm(mix[:, h * rank:(h + 1) * rank], wuv_ref[h]) for h in range(mla_heads)], axis=1)
    x2 = x_ref[...] + _mm(mix, wo_ref[...])
    hn = _rms(x2, nf_ref[...]).astype(BF16)
    acc = x2
    dff = up_ref.shape[1]
    fc = min(FF_CHUNK, dff)
    for cidx in range(dff // fc):
        hid = jnp.dot(hn, up_ref[:, cidx * fc:(cidx + 1) * fc], preferred_element_type=F32)
        hid = jnp.square(jnp.maximum(hid, 0.0)).astype(BF16)
        acc = acc + jnp.dot(hid, down_ref[cidx * fc:(cidx + 1) * fc, :], preferred_element_type=F32)
    if final_norm:
        acc = _rms(acc, fn_ref[...])
    y_ref[...] = acc


def _post(mix, x, wo, nf, up, down, wuv=None, fn=None):
    n, d = x.shape
    tm = min(ROW_TILE, n)
    assert n % tm == 0
    ins = [mix, x]
    specs = [pl.BlockSpec((tm, mix.shape[1]), lambda i: (i, 0)), pl.BlockSpec((tm, d), lambda i: (i, 0))]

    def add_const(arr):
        ins.append(arr)
        specs.append(pl.BlockSpec(arr.shape, lambda i, _nd=arr.ndim: (0,) * _nd,
                                  pipeline_mode=pl.Buffered(1)))

    if wuv is not None:
        add_const(wuv)
    for arr in (wo, nf.reshape(1, d).astype(F32), up, down):
        add_const(arr)
    if fn is not None:
        add_const(fn.reshape(1, d).astype(F32))
    return pl.pallas_call(
        functools.partial(_post_kernel, 0 if wuv is None else wuv.shape[0], fn is not None),
        grid=(n // tm,),
        in_specs=specs,
        out_specs=pl.BlockSpec((tm, d), lambda i: (i, 0)),
        out_shape=jax.ShapeDtypeStruct((n, d), F32),
        compiler_params=_params(("parallel",)),
    )(*ins)


def _mla_pre_kernel(heads, scale, x_ref, cos_ref, sin_ref, kvn_ref, wdkv_ref, latn_ref, wkr_ref, wkrp_ref,
                    nmix_ref, wdq_ref, qn_ref, wuqn_ref, wuqr_ref, wuqrp_ref, wukt_ref,
                    lat_ref, kr_ref, kc_ref, qc_ref):
    x = x_ref[...]
    cos = cos_ref[...]
    sin = sin_ref[...]
    rows = x.shape[0]
    rank = wukt_ref.shape[2]
    kw = kc_ref.shape[1]
    pad = kw - rank - QK_ROPE
    kv_in = _rms(x, kvn_ref[...]).astype(BF16)
    lat = _rms(jnp.dot(kv_in, wdkv_ref[...], preferred_element_type=F32), latn_ref[...])
    kr = (jnp.dot(kv_in, wkr_ref[...], preferred_element_type=F32) * cos[:, :QK_ROPE]
          + jnp.dot(kv_in, wkrp_ref[...], preferred_element_type=F32) * sin[:, :QK_ROPE])
    lat_ref[...] = lat
    kr_ref[...] = kr
    kc_ref[:, :rank] = lat.astype(BF16)
    kc_ref[:, rank:rank + QK_ROPE] = kr.astype(BF16)
    if pad:
        kc_ref[:, rank + QK_ROPE:] = jnp.zeros((rows, pad), BF16)
    xn = _rms(x, nmix_ref[...]).astype(BF16)
    cq = _rms(jnp.dot(xn, wdq_ref[...], preferred_element_type=F32), qn_ref[...]).astype(BF16)
    qn = jnp.dot(cq, wuqn_ref[...], preferred_element_type=F32)
    qr = ((jnp.dot(cq, wuqr_ref[...], preferred_element_type=F32) * cos
           + jnp.dot(cq, wuqrp_ref[...], preferred_element_type=F32) * sin) * scale).astype(BF16)
    for h in range(heads):
        ql = _mm(qn[:, h * QK_NOPE:(h + 1) * QK_NOPE], wukt_ref[h])
        qc_ref[:, h * kw:h * kw + rank] = (ql * scale).astype(BF16)
        qc_ref[:, h * kw + rank:h * kw + rank + QK_ROPE] = qr[:, h * QK_ROPE:(h + 1) * QK_ROPE]
        if pad:
            qc_ref[:, h * kw + rank + QK_ROPE:(h + 1) * kw] = jnp.zeros((rows, pad), BF16)


def _rope_tables(pos, heads):
    half = QK_ROPE // 2
    inv_freq = ROPE_BASE ** (-jnp.arange(half, dtype=F32) / half)
    ang = pos.astype(F32)[:, None] * inv_freq[None, :]
    cos, sin = jnp.cos(ang), jnp.sin(ang)
    cos2 = jnp.concatenate([cos, cos], axis=1)
    sin2 = jnp.concatenate([-sin, sin], axis=1)
    return jnp.tile(cos2, (1, heads)), jnp.tile(sin2, (1, heads))


def _swap_halves(wcols):
    half = QK_ROPE // 2
    return jnp.concatenate([wcols[..., half:], wcols[..., :half]], axis=-1)


def _mla_weights(p, j):
    d, rank = p['w_dkv'].shape
    heads = p['w_uk'].shape[1]
    qrank = p['w_dq'].shape[2]
    wuq = p['w_uq'][j]
    wuq_n = wuq[:, :, :QK_NOPE].reshape(qrank, heads * QK_NOPE)
    wuq_r = wuq[:, :, QK_NOPE:]
    row2 = lambda t: t.reshape(1, -1).astype(F32)
    return dict(
        kvn=row2(p['kv_norm']), wdkv=p['w_dkv'].astype(BF16), latn=row2(p['lat_norm']),
        wkr=p['w_kr'].astype(BF16), wkrp=_swap_halves(p['w_kr']).astype(BF16),
        nmix=row2(p['norm_mix'][p['rw_wr'].shape[0] + j]), wdq=p['w_dq'][j].astype(BF16), qn=row2(p['q_norm'][j]),
        wuqn=wuq_n.astype(BF16), wuqr=wuq_r.reshape(qrank, heads * QK_ROPE).astype(BF16),
        wuqrp=_swap_halves(wuq_r).reshape(qrank, heads * QK_ROPE).astype(BF16),
        wukt=jnp.transpose(p['w_uk'], (1, 2, 0)).astype(BF16),
        wuv=jnp.transpose(p['w_uv'], (1, 0, 2)).astype(BF16),
        heads=heads, rank=rank)


def _key_width(rank):
    return -(-(rank + QK_ROPE) // 128) * 128


def _mla_pre(x, cos, sin, mw):
    n, d = x.shape
    heads, rank = mw['heads'], mw['rank']
    kw = _key_width(rank)
    tm = min(ROW_TILE, n)
    assert n % tm == 0 and cos.shape[0] % tm == 0
    tab_blocks = cos.shape[0] // tm
    row = lambda width: pl.BlockSpec((tm, width), lambda i: (i, 0))
    tab = pl.BlockSpec((tm, heads * QK_ROPE), lambda i: (i % tab_blocks, 0))
    names = ('kvn', 'wdkv', 'latn', 'wkr', 'wkrp', 'nmix', 'wdq', 'qn', 'wuqn', 'wuqr', 'wuqrp', 'wukt')
    consts = [mw[k] for k in names]
    scale = float((QK_NOPE + QK_ROPE) ** -0.5)
    return pl.pallas_call(
        functools.partial(_mla_pre_kernel, heads, scale),
        grid=(n // tm,),
        in_specs=[row(d), tab, tab] + [_const_spec(a.shape) for a in consts],
        out_specs=[row(rank), row(QK_ROPE), row(kw), row(heads * kw)],
        out_shape=[jax.ShapeDtypeStruct((n, rank), F32), jax.ShapeDtypeStruct((n, QK_ROPE), F32),
                   jax.ShapeDtypeStruct((n, kw), BF16), jax.ShapeDtypeStruct((n, heads * kw), BF16)],
        compiler_params=_params(("parallel",)),
    )(x, cos, sin, *consts)


def _softmax_update(s, vals, m_scr, l_scr, acc_scr):
    lanes = m_scr.shape[1]
    width = s.shape[1]
    m_prev = m_scr[...]
    m_new = jnp.maximum(m_prev, jnp.max(s, axis=-1, keepdims=True))
    alpha = jnp.exp(m_prev - m_new)
    m_wide = m_new[:, :width] if width <= lanes else jnp.tile(m_new, (1, width // lanes))
    p = jnp.exp(s - m_wide)
    l_scr[...] = alpha * l_scr[...] + jnp.sum(p, axis=-1, keepdims=True)
    acc_scr[...] = (jnp.tile(alpha, (1, acc_scr.shape[1] // lanes)) * acc_scr[...]
                    + jnp.dot(p.astype(BF16), vals, preferred_element_type=F32))
    m_scr[...] = m_new


def _softmax_init(m_scr, l_scr, acc_scr):
    m_scr[...] = jnp.full(m_scr.shape, MASK_VALUE, F32)
    l_scr[...] = jnp.zeros(l_scr.shape, F32)
    acc_scr[...] = jnp.zeros(acc_scr.shape, F32)


def _softmax_result(l_scr, acc_scr):
    return acc_scr[...] / jnp.tile(l_scr[...], (1, acc_scr.shape[1] // l_scr.shape[1]))


def _attn_prompt_kernel(heads, rank, tk, qc_ref, kc_ref, pkc_ref, o_ref, m_scr, l_scr, acc_scr):
    i = pl.program_id(1)
    tq = qc_ref.shape[0]
    kw = kc_ref.shape[2]
    group = heads // ATTN_ROW_PARTS
    rows = group * tq
    qcs = [jnp.concatenate([qc_ref[:, h * kw:(h + 1) * kw] for h in range(g * group, (g + 1) * group)], axis=0)
           for g in range(ATTN_ROW_PARTS)]
    stats = [(m_scr.at[pl.ds(g * rows, rows)], l_scr.at[pl.ds(g * rows, rows)], acc_scr.at[pl.ds(g * rows, rows)])
             for g in range(ATTN_ROW_PARTS)]
    _softmax_init(m_scr, l_scr, acc_scr)
    pkc = pkc_ref[...]
    for qc, st in zip(qcs, stats):
        _softmax_update(_mm_nt(qc, pkc), pkc[:, :rank], *st)
    n_full = (i * tq) // tk

    def body(jb, carry):
        kc = kc_ref[0, pl.ds(pl.multiple_of(jb * tk, tk), tk), :]
        for qc, st in zip(qcs, stats):
            _softmax_update(_mm_nt(qc, kc), kc[:, :rank], *st)
        return carry

    lax.fori_loop(0, n_full, body, 0)
    kc = kc_ref[0, pl.ds(pl.multiple_of(n_full * tk, tk), tk), :]
    q_pos = i * tq + lax.broadcasted_iota(jnp.int32, (rows, tk), 0) % tq
    k_pos = n_full * tk + lax.broadcasted_iota(jnp.int32, (rows, tk), 1)
    for qc, st in zip(qcs, stats):
        _softmax_update(jnp.where(k_pos <= q_pos, _mm_nt(qc, kc), MASK_VALUE), kc[:, :rank], *st)
    o = _softmax_result(l_scr, acc_scr).astype(BF16)
    for h in range(heads):
        o_ref[:, h * rank:(h + 1) * rank] = o[h * tq:(h + 1) * tq, :]


def _attn_prompt(qc, kc, pkc, batch, heads, rank):
    n = qc.shape[0]
    t = n // batch
    kw = kc.shape[1]
    tq = min(ATTN_TILE, t)
    tk = min(ATTN_KEY_TILE, t)
    assert t % tk == 0 and tk % tq == 0
    nq = t // tq
    return pl.pallas_call(
        functools.partial(_attn_prompt_kernel, heads, rank, tk),
        grid=(batch, nq),
        in_specs=[pl.BlockSpec((tq, heads * kw), lambda b, i: (b * nq + i, 0)),
                  pl.BlockSpec((1, t, kw), lambda b, i: (b, 0, 0)),
                  _const_spec(pkc.shape)],
        out_specs=pl.BlockSpec((tq, heads * rank), lambda b, i: (b * nq + i, 0)),
        out_shape=jax.ShapeDtypeStruct((n, heads * rank), BF16),
        scratch_shapes=[pltpu.VMEM((heads * tq, 128), F32), pltpu.VMEM((heads * tq, 128), F32),
                        pltpu.VMEM((heads * tq, rank), F32)],
        compiler_params=_params(("parallel", "arbitrary")),
    )(qc, kc.reshape(batch, t, kw), pkc)


def _attn_sample_kernel(chunk_pages, seq, rank, pt_ref, qc_ref, nkc_ref, lat_hbm, krt_hbm, o_ref,
                        lat_buf, krt_buf, sems, m_scr, l_scr, acc_scr):
    b = pl.program_id(0)
    qc = qc_ref[0]
    ahead = SAMPLE_SLOTS - 1
    total_chunks = pl.num_programs(0) * SAMPLE_CHUNKS

    def page_copies(number, p):
        slot = lax.rem(number, SAMPLE_SLOTS)
        page = pt_ref[number // SAMPLE_CHUNKS, lax.rem(number, SAMPLE_CHUNKS) * chunk_pages + p]
        rows = pl.ds(pl.multiple_of(p * PAGE_SIZE, PAGE_SIZE), PAGE_SIZE)
        return (pltpu.make_async_copy(lat_hbm.at[page], lat_buf.at[slot, rows, :], sems.at[slot, 0]),
                pltpu.make_async_copy(krt_hbm.at[page], krt_buf.at[slot, :, rows], sems.at[slot, 1]))

    def start_chunk(number):
        def body(p, carry):
            for cp in page_copies(number, p):
                cp.start()
            return carry
        lax.fori_loop(0, chunk_pages, body, 0, unroll=min(8, chunk_pages))

    def wait_chunk(number):
        def body(p, carry):
            for cp in page_copies(number, p):
                cp.wait()
            return carry
        lax.fori_loop(0, chunk_pages, body, 0, unroll=min(8, chunk_pages))

    @pl.when(b == 0)
    def _():
        for number in range(ahead):
            start_chunk(jnp.int32(number))

    _softmax_init(m_scr, l_scr, acc_scr)
    for chunk in range(SAMPLE_CHUNKS):
        number = b * SAMPLE_CHUNKS + chunk
        wait_chunk(number)

        @pl.when(number + ahead < total_chunks)
        def _():
            start_chunk(number + ahead)

        slot = lax.rem(number, SAMPLE_SLOTS)
        kl = lat_buf[slot].astype(BF16)
        krt = krt_buf[slot].astype(BF16)
        s = _mm_nt(qc[:, :rank], kl) + jnp.dot(qc[:, rank:rank + QK_ROPE], krt, preferred_element_type=F32)
        _softmax_update(s, kl, m_scr, l_scr, acc_scr)

    nkc = nkc_ref[0]
    s = _mm_nt(qc, nkc)
    q_t = lax.broadcasted_iota(jnp.int32, s.shape, 0) % seq
    k_t = lax.broadcasted_iota(jnp.int32, s.shape, 1)
    s = jnp.where(k_t <= q_t, s, MASK_VALUE)
    _softmax_update(s, nkc[:, :rank], m_scr, l_scr, acc_scr)
    o_ref[0] = _softmax_result(l_scr, acc_scr).astype(BF16)


def _attn_sample(qc, nkc, seq, rank, cache_latent, cache_krope_t, page_table):
    b, rows, kw = qc.shape
    total_pages = page_table.shape[1]
    assert total_pages % SAMPLE_CHUNKS == 0 and 2 <= SAMPLE_SLOTS <= b * SAMPLE_CHUNKS
    chunk_pages = total_pages // SAMPLE_CHUNKS
    chunk_keys = chunk_pages * PAGE_SIZE
    per_b = lambda arr: pl.BlockSpec((1,) + arr.shape[1:], lambda bi, pt: (bi, 0, 0))
    grid_spec = pltpu.PrefetchScalarGridSpec(
        num_scalar_prefetch=1,
        grid=(b,),
        in_specs=[per_b(qc), per_b(nkc), pl.BlockSpec(memory_space=pl.ANY), pl.BlockSpec(memory_space=pl.ANY)],
        out_specs=pl.BlockSpec((1, rows, rank), lambda bi, pt: (bi, 0, 0)),
        scratch_shapes=[pltpu.VMEM((SAMPLE_SLOTS, chunk_keys, rank), F32),
                        pltpu.VMEM((SAMPLE_SLOTS, QK_ROPE, chunk_keys), F32),
                        pltpu.SemaphoreType.DMA((SAMPLE_SLOTS, 2)),
                        pltpu.VMEM((rows, 128), F32), pltpu.VMEM((rows, 128), F32), pltpu.VMEM((rows, rank), F32)],
    )
    return pl.pallas_call(
        functools.partial(_attn_sample_kernel, chunk_pages, seq, rank),
        grid_spec=grid_spec,
        out_shape=jax.ShapeDtypeStruct((b, rows, rank), BF16),
        compiler_params=_params(("arbitrary",)),
    )(page_table, qc, nkc, cache_latent, cache_krope_t)


def _layer0(x, shift0, wkv0, p, flat):
    b, t, d = x.shape
    weights = _rwkv_pre_weights(p, 0)
    pre = _rwkv_pre_flat if flat else _rwkv_pre_seq
    seven, shift_new = pre(x, shift0, p['norm_mix'][0], weights)
    scan = _wkv_lanes if flat else _wkv_scan
    og, wkv_new = scan(seven, p['rw_rk'][0], p['rw_lnx_g'][0], p['rw_lnx_b'][0], wkv0)
    x1 = _post(og.reshape(b * t, d), x.reshape(b * t, d), p['rw_wo'][0].astype(BF16), p['norm_ffn'][0],
               p['ffn_up'][0].astype(BF16), p['ffn_down'][0].astype(BF16))
    return x1, wkv_new, shift_new


def _layer1_post(olat, x1, p, mw):
    return _post(olat, x1, p['w_o_mla'][0].astype(BF16), p['norm_ffn'][1], p['ffn_up'][1].astype(BF16),
                 p['ffn_down'][1].astype(BF16), wuv=mw['wuv'], fn=p['norm_final'])


def kernel(x_prompt, x_sample, state_wkv, state_shift, cache_latent, cache_krope, page_table,
           meta_tokens, rw_mu, rw_wr, rw_wk, rw_wv, rw_wo, rw_w0, rw_w1, rw_w2, rw_a0, rw_a1, rw_a2,
           rw_g1, rw_g2, rw_kk, rw_ka, rw_rk, rw_lnx_g, rw_lnx_b, norm_mix, norm_ffn, ffn_up, ffn_down,
           kv_norm, w_dkv, lat_norm, w_kr, w_uk, w_uv, w_dq, q_norm, w_uq, w_o_mla, norm_final):
    p = dict(rw_mu=rw_mu, rw_wr=rw_wr, rw_wk=rw_wk, rw_wv=rw_wv, rw_wo=rw_wo, rw_w0=rw_w0, rw_w1=rw_w1,
             rw_w2=rw_w2, rw_a0=rw_a0, rw_a1=rw_a1, rw_a2=rw_a2, rw_g1=rw_g1, rw_g2=rw_g2, rw_kk=rw_kk,
             rw_ka=rw_ka, rw_rk=rw_rk, rw_lnx_g=rw_lnx_g, rw_lnx_b=rw_lnx_b, norm_mix=norm_mix,
             norm_ffn=norm_ffn, ffn_up=ffn_up, ffn_down=ffn_down, kv_norm=kv_norm, w_dkv=w_dkv,
             lat_norm=lat_norm, w_kr=w_kr, w_uk=w_uk, w_uv=w_uv, w_dq=w_dq, q_norm=q_norm, w_uq=w_uq,
             w_o_mla=w_o_mla, norm_final=norm_final)
    assert rw_wr.shape[0] == 1 and w_dq.shape[0] == 1, "one RWKV layer followed by one MLA layer"
    bsz, seq, d = x_prompt.shape
    dbsz, dseq, _ = x_sample.shape
    rw_heads = d // RW_HEAD
    mw = _mla_weights(p, 0)
    heads, rank = mw['heads'], mw['rank']
    n_meta = meta_tokens.shape[0]
    past_len = page_table.shape[1] * PAGE_SIZE

    xm1, wkv_m, shift_m = _layer0(meta_tokens[None].astype(F32), jnp.zeros((1, d), F32),
                                  jnp.zeros((1, rw_heads, RW_HEAD, RW_HEAD), F32), p, flat=False)
    cos_m, sin_m = _rope_tables(jnp.arange(n_meta, dtype=jnp.int32), heads)
    lat_m, kr_m, kc_m, _ = _mla_pre(xm1, cos_m, sin_m, mw)

    xp1, wkv_p, shift_p = _layer0(x_prompt, jnp.broadcast_to(shift_m, (bsz, d)),
                                  jnp.broadcast_to(wkv_m, (bsz,) + wkv_m.shape[1:]), p, flat=False)
    cos_p, sin_p = _rope_tables(n_meta + jnp.arange(seq, dtype=jnp.int32), heads)
    lat_p, kr_p, kc_p, qc_p = _mla_pre(xp1, cos_p, sin_p, mw)
    olat_p = _attn_prompt(qc_p, kc_p, kc_m, bsz, heads, rank)
    y_prompt = _layer1_post(olat_p, xp1, p, mw).reshape(bsz, seq, d)
    latent_prompt = jnp.concatenate(
        [jnp.broadcast_to(lat_m[None], (bsz, n_meta, rank)), lat_p.reshape(bsz, seq, rank)], axis=1)
    krope_prompt = jnp.concatenate(
        [jnp.broadcast_to(kr_m[None], (bsz, n_meta, QK_ROPE)), kr_p.reshape(bsz, seq, QK_ROPE)], axis=1)

    xs1, wkv_s, shift_s = _layer0(x_sample, state_shift[0], jnp.transpose(state_wkv[0], (1, 2, 3, 0)), p, flat=True)
    wkv_s = jnp.transpose(wkv_s, (3, 0, 1, 2))
    cos_s, sin_s = _rope_tables(past_len + jnp.arange(dseq, dtype=jnp.int32), heads)
    cos_s, sin_s = jnp.tile(cos_s, (dbsz, 1)), jnp.tile(sin_s, (dbsz, 1))
    lat_s, kr_s, kc_s, qc_s = _mla_pre(xs1, cos_s, sin_s, mw)
    kw = kc_s.shape[1]
    qc_s = jnp.transpose(qc_s.reshape(dbsz, dseq, heads, kw), (0, 2, 1, 3)).reshape(dbsz, heads * dseq, kw)
    new_rows = -(-dseq // 16) * 16
    nkc_s = jnp.pad(kc_s.reshape(dbsz, dseq, kw), ((0, 0), (0, new_rows - dseq), (0, 0)))
    olat_s = _attn_sample(qc_s, nkc_s, dseq, rank, cache_latent, jnp.swapaxes(cache_krope, 1, 2), page_table)
    olat_s = jnp.transpose(olat_s.reshape(dbsz, heads, dseq, rank), (0, 2, 1, 3)).reshape(dbsz * dseq, heads * rank)
    y_sample = _layer1_post(olat_s, xs1, p, mw).reshape(dbsz, dseq, d)

    return (y_prompt, y_sample, wkv_p[None], shift_p[None], latent_prompt, krope_prompt,
            wkv_s[None], shift_s[None], lat_s.reshape(dbsz, dseq, rank), kr_s.reshape(dbsz, dseq, QK_ROPE))
```

```python
import functools
import math

import jax
import jax.numpy as jnp
from jax import lax
from jax.experimental import pallas as pl
from jax.experimental.pallas import tpu as pltpu

F32 = jnp.float32
BF16 = jnp.bfloat16

RW_HEAD = 64
QK_NOPE = 128
QK_ROPE = 64
V_HEAD = 128
N_META = 16
PAGE_SIZE = 128
NORM_EPS = 1e-6
GN_EPS = 64e-5
ROPE_BASE = 10000.0
DECAY_SCALE = math.exp(-0.5)
MASK_VALUE = -1e30

V7X_VMEM_LIMIT_BYTES = 56 * 1024 * 1024
SCAN_CHUNK = 64
SCAN_MIN_CHUNK = 16
SCAN_SEQS_PER_STEP = 2
ROW_TILE = 256
POST_ROW_TILE = 512
FF_CHUNK = 1024
ATTN_TILE = 256
ATTN_KEY_TILE = 1024
ATTN_DIAG_TILE = 512
ATTN_ROW_PARTS = 4
SAMPLE_CHUNKS = 2
SAMPLE_SLOTS = 3


def _mm(a, b):
    return jnp.dot(a.astype(BF16), b.astype(BF16), preferred_element_type=F32)


def _mm_nt(a, b):
    return lax.dot_general(a.astype(BF16), b.astype(BF16), (((1,), (1,)), ((), ())),
                           preferred_element_type=F32)


def _split3(x):
    h = x.astype(BF16)
    r1 = x - h.astype(F32)
    m = r1.astype(BF16)
    lo = (r1 - m.astype(F32)).astype(BF16)
    return h, m, lo


def _mm_exact_lhs(sel, x):
    h, m, lo = _split3(x)
    sel = sel.astype(BF16)
    dot = lambda t: jnp.dot(sel, t, preferred_element_type=F32)
    return dot(h) + (dot(m) + dot(lo))


def _transpose_f32(x):
    n = x.shape[1]
    eye = (lax.broadcasted_iota(jnp.int32, (n, n), 0) == lax.broadcasted_iota(jnp.int32, (n, n), 1))
    eye = eye.astype(BF16)
    h, m, lo = _split3(x)
    dot = lambda t: lax.dot_general(eye, t, (((1,), (1,)), ((), ())), preferred_element_type=F32)
    return dot(h) + (dot(m) + dot(lo))


def _rms(x, g):
    return x * lax.rsqrt(jnp.mean(x * x, axis=-1, keepdims=True) + NORM_EPS) * g


def _sigmoid(z):
    return 1.0 / (1.0 + jnp.exp(-z))


def _const_spec(shape):
    nd = len(shape)
    return pl.BlockSpec(shape, lambda *_: (0,) * nd)


def _params(sem):
    return pltpu.CompilerParams(dimension_semantics=sem, vmem_limit_bytes=V7X_VMEM_LIMIT_BYTES)


def _rwkv_pre_math(xn, prev, w, outs):
    (mu, wr, wk, wv, w0, w1, w2, a0, a1, a2, g1, g2, kkp, kap) = w
    r_ref, w_ref, k_ref, v_ref, kk_ref, a_ref, g_ref = outs
    xx = prev - xn
    mix = lambda m: (xn + xx * mu[m:m + 1, :]).astype(BF16)
    r = _mm(mix(0), wr[...])
    z = w0[...] + _mm(jnp.tanh(_mm(mix(1), w1[...])), w2[...])
    wdec = -DECAY_SCALE * _sigmoid(z)
    k = _mm(mix(2), wk[...])
    v = _mm(mix(3), wv[...])
    a = _sigmoid(a0[...] + _mm(_mm(mix(4), a1[...]), a2[...]))
    g = _mm(_sigmoid(_mm(mix(5), g1[...])), g2[...])
    r_ref[...] = r.reshape(r_ref.shape)
    w_ref[...] = wdec.reshape(w_ref.shape)
    k_ref[...] = (k * (1.0 + (a - 1.0) * kap[...])).reshape(k_ref.shape)
    v_ref[...] = v.reshape(v_ref.shape)
    kk_ref[...] = (k * kkp[...]).reshape(kk_ref.shape)
    a_ref[...] = a.reshape(a_ref.shape)
    g_ref[...] = g.reshape(g_ref.shape)


def _rwkv_pre_seq_kernel(x_ref, halo_ref, s0_ref, gmix_ref, *rest):
    w, outs, tail_ref = rest[:14], rest[14:21], rest[21]
    j = pl.program_id(1)
    x = x_ref[0]
    tm = x.shape[0]
    gm = gmix_ref[...]
    xn = _rms(x, gm)
    row = lax.broadcasted_iota(jnp.int32, (tm, 1), 0)
    prev_raw = jnp.where(row == 0, halo_ref[0][7:8, :], pltpu.roll(x, 1, 0))
    prev = jnp.where((row == 0) & (j == 0), s0_ref[0], _rms(prev_raw, gm))
    tail_ref[0] = xn[tm - 8:, :]
    _rwkv_pre_math(xn, prev, w, outs)


def _rwkv_pre_flat_kernel(seq_len, x_ref, xs_ref, s0x_ref, gmix_ref, *rest):
    w, outs, xn_ref = rest[:14], rest[14:21], rest[21]
    x = x_ref[...]
    tm = x.shape[0]
    gm = gmix_ref[...]
    xn = _rms(x, gm)
    row = lax.broadcasted_iota(jnp.int32, (tm, 1), 0)
    prev = jnp.where(row % seq_len == 0, s0x_ref[...], _rms(xs_ref[...], gm))
    xn_ref[...] = xn
    _rwkv_pre_math(xn, prev, w, outs)


def _rwkv_pre_weights(p, i):
    row2 = lambda t: t.reshape(1, -1).astype(F32)
    return (p['rw_mu'][i].astype(F32), p['rw_wr'][i].astype(BF16), p['rw_wk'][i].astype(BF16),
            p['rw_wv'][i].astype(BF16), row2(p['rw_w0'][i]), p['rw_w1'][i].astype(BF16),
            p['rw_w2'][i].astype(BF16), row2(p['rw_a0'][i]), p['rw_a1'][i].astype(BF16),
            p['rw_a2'][i].astype(BF16), p['rw_g1'][i].astype(BF16), p['rw_g2'][i].astype(BF16),
            row2(p['rw_kk'][i]), row2(p['rw_ka'][i]))


def _rwkv_pre_seq(x, shift0, gmix, weights):
    b, t, d = x.shape
    tm = min(ROW_TILE, t)
    assert t % tm == 0 and tm % 8 == 0
    blk = pl.BlockSpec((1, tm, d), lambda bi, j: (bi, j, 0))
    halo = pl.BlockSpec((1, 8, d), lambda bi, j: (bi, jnp.maximum(j * (tm // 8) - 1, 0), 0))
    s0 = pl.BlockSpec((1, 1, d), lambda bi, j: (bi, 0, 0))
    tail = pl.BlockSpec((1, 8, d), lambda bi, j: (bi, 0, 0))
    out = pl.pallas_call(
        _rwkv_pre_seq_kernel,
        grid=(b, t // tm),
        in_specs=[blk, halo, s0, _const_spec((1, d))] + [_const_spec(w.shape) for w in weights],
        out_specs=[blk] * 7 + [tail],
        out_shape=[jax.ShapeDtypeStruct((b, t, d), F32)] * 7 + [jax.ShapeDtypeStruct((b, 8, d), F32)],
        compiler_params=_params(("parallel", "arbitrary")),
    )(x, x, shift0.reshape(b, 1, d), gmix.reshape(1, d), *weights)
    return out[:7], out[7][:, 7, :]


def _rwkv_pre_flat(x, shift0, gmix, weights):
    b, t, d = x.shape
    n = b * t
    tm = min(ROW_TILE, n)
    assert n % tm == 0 and tm % t == 0
    xf = x.reshape(n, d)
    xs = jnp.concatenate([jnp.zeros((1, d), F32), xf[:-1]], axis=0)
    s0x = jnp.concatenate([shift0[:, None, :], jnp.zeros((b, t - 1, d), F32)], axis=1).reshape(n, d)
    blk = pl.BlockSpec((tm, d), lambda i: (i, 0))
    out = pl.pallas_call(
        functools.partial(_rwkv_pre_flat_kernel, t),
        grid=(n // tm,),
        in_specs=[blk, blk, blk, _const_spec((1, d))] + [_const_spec(w.shape) for w in weights],
        out_specs=[blk] * 8,
        out_shape=[jax.ShapeDtypeStruct((n, d), F32)] * 8,
        compiler_params=_params(("parallel",)),
    )(xf, xs, s0x, gmix.reshape(1, d), *weights)
    seven = [o.reshape(b, t, d) for o in out[:7]]
    return seven, out[7].reshape(b, t, d)[:, t - 1, :]


def _wkv_chunk_kernel(nb, tc, c, heads, r_ref, w_ref, k_ref, v_ref, kk_ref, a_ref, g_ref,
                      rk_ref, lng_ref, lnb_ref, s0_ref, og_ref, sout_ref, h_scr):
    step = pl.program_id(1)
    n = RW_HEAD
    pw_lanes = 2 * n
    npairs = heads // 2
    pairs = [(e, p) for e in range(nb) for p in range(npairs)]
    items = [(e, p, hh) for e, p in pairs for hh in range(2)]
    zeros_nn = jnp.zeros((n, n), F32)

    @pl.when(step == 0)
    def _():
        for i, (e, p) in enumerate(pairs):
            t0 = _transpose_f32(s0_ref[e, 2 * p])
            t1 = _transpose_f32(s0_ref[e, 2 * p + 1])
            h_scr[i] = jnp.concatenate([jnp.concatenate([t0, zeros_nn], axis=1),
                                        jnp.concatenate([zeros_nn, t1], axis=1)], axis=0)

    iota = lambda shape, dim: lax.broadcasted_iota(jnp.int32, shape, dim)
    tril_incl = iota((c, c), 0) >= iota((c, c), 1)
    tril_strict = iota((c, c), 0) > iota((c, c), 1)
    eye_c = (iota((c, c), 0) == iota((c, c), 1)).astype(F32)
    uk_mask = (iota((c, 2 * c), 1) >= c) & (iota((c, 2 * c), 0) > iota((c, 2 * c), 1) - c)
    ar_mask = iota((c, 2 * c), 0) >= iota((c, 2 * c), 1) % c
    head_of = lambda rows: iota((rows, pw_lanes), 1) >= n
    second_wide = lambda rows: iota((rows, 2 * pw_lanes), 1) % pw_lanes >= n
    eye_pair = iota((pw_lanes, pw_lanes), 0) == iota((pw_lanes, pw_lanes), 1)
    pick = [iota((n, pw_lanes), 1) == iota((n, pw_lanes), 0) + n * hh for hh in range(2)]
    zeros_c = jnp.zeros((c, pw_lanes), BF16)

    def half_sums(x):
        second = head_of(c)
        return jnp.where(second, jnp.sum(jnp.where(second, x, 0.0), axis=-1, keepdims=True),
                         jnp.sum(jnp.where(second, 0.0, x), axis=-1, keepdims=True))

    def token_terms(e):
        def load(ref):
            x = ref[e]
            if tc < c:
                x = jnp.concatenate([x, jnp.zeros((c - tc, x.shape[1]), F32)], axis=0)
            return x

        r, w, k, v, kk, a, g = (load(t) for t in (r_ref, w_ref, k_ref, v_ref, kk_ref, a_ref, g_ref))
        kk = jnp.concatenate(
            [x * jnp.minimum(lax.rsqrt(half_sums(x * x)), 1e12)
             for x in (kk[:, p * pw_lanes:(p + 1) * pw_lanes] for p in range(npairs))], axis=1)
        lw = _mm_exact_lhs(tril_incl, w)
        lw_end = lw[c - 1:c, :]
        p_inv = jnp.exp(-lw)
        p_tail = jnp.exp(lw_end - lw)
        kb = kk * a
        rt = r * jnp.exp(lw)
        return dict(
            v=v, g=g, rt=rt, p_end=jnp.exp(lw_end), bonus_rk=r * k * rk_ref[...],
            kt=(kk * jnp.exp(lw - w)).astype(BF16), rt_b=rt.astype(BF16), nbt=(-(kb * p_inv)).astype(BF16),
            kti=(k * p_inv).astype(BF16), nbh=(-(kb * p_tail)).astype(BF16), kh=(k * p_tail).astype(BF16),
            vb=v.astype(BF16))

    terms = [token_terms(e) for e in range(nb)]
    col = lambda name, e, p: terms[e][name][:, p * pw_lanes:(p + 1) * pw_lanes]
    of_pair = lambda xs, e, p: [x for x, it in zip(xs, items) if it[:2] == (e, p)]
    by_head = lambda rows, x0, x1: jnp.where(head_of(rows), x1, x0)

    own = lambda rows, hh: head_of(rows) if hh else jnp.logical_not(head_of(rows))
    aa = [_mm_nt(jnp.where(own(2 * c, hh), jnp.concatenate([col('kt', e, p), col('rt_b', e, p)], axis=0), 0.0),
                 jnp.concatenate([col('nbt', e, p), col('kti', e, p)], axis=0))
          for e, p, hh in items]
    a_ub = [jnp.where(tril_strict, x[:c, :c], 0.0) for x in aa]
    a_uk = [jnp.where(uk_mask, x[:c, :], 0.0).astype(BF16) for x in aa]
    a_r = [jnp.where(ar_mask, x[c:, :], 0.0).astype(BF16) for x in aa]
    tinv = [eye_c + x for x in a_ub]
    pw = [x.astype(BF16) for x in a_ub]
    pw = [jnp.dot(x, x, preferred_element_type=F32).astype(BF16) for x in pw]
    span = 2
    while 2 * span < c:
        prod = [jnp.dot(x, jnp.concatenate([t.astype(BF16), x], axis=1), preferred_element_type=F32)
                for x, t in zip(pw, tinv)]
        tinv = [t + y[:, :c] for t, y in zip(tinv, prod)]
        pw = [y[:, c:].astype(BF16) for y in prod]
        span *= 2
    tinv = [t + jnp.dot(x, t.astype(BF16), preferred_element_type=F32) for x, t in zip(pw, tinv)]
    av = [jnp.dot(x, jnp.concatenate([zeros_c, col('vb', e, p)], axis=0), preferred_element_type=F32)
          for x, (e, p, hh) in zip(a_uk, items)]
    wu = [_mm(t, jnp.concatenate([col('kt', e, p), x.astype(BF16)], axis=1))
          for t, x, (e, p, hh) in zip(tinv, av, items)]
    zt = [_mm_nt(pick[hh], jnp.concatenate([col('nbh', e, p), col('kh', e, p)], axis=0))
          for e, p, hh in items]
    rhs2 = []
    for e, p in pairs:
        w0, w1 = of_pair(wu, e, p)
        w_pair = by_head(c, w0[:, :pw_lanes], w1[:, :pw_lanes]).astype(BF16)
        u_pair = by_head(c, w0[:, pw_lanes:], w1[:, pw_lanes:]).astype(BF16)
        rhs2.append(jnp.concatenate([jnp.concatenate([w_pair, zeros_c], axis=0),
                                     jnp.concatenate([u_pair, col('vb', e, p)], axis=0)], axis=1))
    res = [_mm(jnp.concatenate([x, z.astype(BF16)], axis=0), rhs2[pairs.index((e, p))])
           for x, z, (e, p, hh) in zip(a_r, zt, items)]
    for i, (e, p) in enumerate(pairs):
        r0, r1 = of_pair(res, e, p)
        top = jnp.where(second_wide(c), r1[:c], r0[:c])
        low = jnp.concatenate([jnp.where(second_wide(n), 0.0, r0[c:]),
                               jnp.where(second_wide(n), r1[c:], 0.0)], axis=0)
        m_bd = jnp.where(eye_pair, col('p_end', e, p), 0.0) + low[:, :pw_lanes]
        ro = _mm(jnp.concatenate([col('rt', e, p) + top[:, :pw_lanes], m_bd], axis=0), h_scr[i])
        o = ro[:c] + top[:, pw_lanes:]
        h_scr[i] = ro[c:] + low[:, pw_lanes:]
        mean = half_sums(o) * (1.0 / n)
        var = half_sums(jnp.square(o - mean)) * (1.0 / n)
        on = (o - mean) * lax.rsqrt(var + GN_EPS)
        bonus = half_sums(col('bonus_rk', e, p)) * col('v', e, p)
        lanes = slice(p * pw_lanes, (p + 1) * pw_lanes)
        out = (on * lng_ref[:, lanes] + lnb_ref[:, lanes] + bonus) * col('g', e, p)
        og_ref[e, :, lanes] = out[:tc]

    @pl.when(step == pl.num_programs(1) - 1)
    def _():
        for i, (e, p) in enumerate(pairs):
            hbd = h_scr[i]
            sout_ref[e, 2 * p] = _transpose_f32(hbd[:n, :n])
            sout_ref[e, 2 * p + 1] = _transpose_f32(hbd[n:, n:])


def _wkv_scan(seven, rk, lng, lnb, state0):
    b, t, d = seven[0].shape
    heads = d // RW_HEAD
    tc = min(SCAN_CHUNK, t)
    c = max(tc, SCAN_MIN_CHUNK)
    nb = SCAN_SEQS_PER_STEP if b % SCAN_SEQS_PER_STEP == 0 else 1
    assert t % tc == 0 and heads % 2 == 0
    blk = pl.BlockSpec((nb, tc, d), lambda bi, j: (bi, j, 0))
    st = pl.BlockSpec((nb, heads, RW_HEAD, RW_HEAD), lambda bi, j: (bi, 0, 0, 0))
    og, s_out = pl.pallas_call(
        functools.partial(_wkv_chunk_kernel, nb, tc, c, heads),
        grid=(b // nb, t // tc),
        in_specs=[blk] * 7 + [_const_spec((1, d))] * 3 + [st],
        out_specs=[blk, st],
        out_shape=[jax.ShapeDtypeStruct((b, t, d), F32),
                   jax.ShapeDtypeStruct((b, heads, RW_HEAD, RW_HEAD), F32)],
        scratch_shapes=[pltpu.VMEM((nb * (heads // 2), 2 * RW_HEAD, 2 * RW_HEAD), F32)],
        compiler_params=_params(("parallel", "arbitrary")),
    )(*seven, rk.reshape(1, d).astype(F32), lng.reshape(1, d).astype(F32),
      lnb.reshape(1, d).astype(F32), state0)
    return og, s_out


def _wkv_lanes_kernel(steps, r_ref, w_ref, k_ref, v_ref, kk_ref, a_ref, g_ref, rk_ref, lng_ref, lnb_ref,
                      s0_ref, og_ref, sout_ref):
    s = s0_ref[0]
    for t in range(steps):
        kk = kk_ref[t]
        kk = kk * jnp.minimum(lax.rsqrt(jnp.sum(kk * kk, axis=0, keepdims=True)), 1e12)
        k = k_ref[t]
        v = v_ref[t]
        r = r_ref[t]
        sa = jnp.sum(s * kk[None], axis=1)
        s = s * jnp.exp(w_ref[t])[None] - sa[:, None, :] * (kk * a_ref[t])[None] + v[:, None, :] * k[None]
        o = jnp.sum(s * r[None], axis=1)
        mean = jnp.mean(o, axis=0, keepdims=True)
        var = jnp.mean(jnp.square(o - mean), axis=0, keepdims=True)
        on = (o - mean) * lax.rsqrt(var + GN_EPS)
        bonus = jnp.sum(r * k * rk_ref[...], axis=0, keepdims=True) * v
        og_ref[t] = (on * lng_ref[...] + lnb_ref[...] + bonus) * g_ref[t]
    sout_ref[0] = s


def _wkv_lanes(seven, rk, lng, lnb, state0_t):
    b, t, d = seven[0].shape
    heads = d // RW_HEAD
    tok = [jnp.transpose(x, (1, 2, 0)) for x in seven]
    col = lambda p: jnp.broadcast_to(p.reshape(d, 1).astype(F32), (d, b))
    blk = pl.BlockSpec((t, RW_HEAD, b), lambda h: (0, h, 0))
    par = pl.BlockSpec((RW_HEAD, b), lambda h: (h, 0))
    st = pl.BlockSpec((1, RW_HEAD, RW_HEAD, b), lambda h: (h, 0, 0, 0))
    og, s_out = pl.pallas_call(
        functools.partial(_wkv_lanes_kernel, t),
        grid=(heads,),
        in_specs=[blk] * 7 + [par] * 3 + [st],
        out_specs=[blk, st],
        out_shape=[jax.ShapeDtypeStruct((t, d, b), F32), jax.ShapeDtypeStruct(state0_t.shape, F32)],
        compiler_params=_params(("parallel",)),
    )(*tok, col(rk), col(lng), col(lnb), state0_t)
    return jnp.transpose(og, (2, 0, 1)), s_out


def _post_kernel(mla_heads, final_norm, mix_ref, x_ref, *rest):
    if mla_heads:
        wuv_ref, rest = rest[0], rest[1:]
    wo_ref, nf_ref, up_ref, down_ref = rest[:4]
    rest = rest[4:]
    if final_norm:
        fn_ref, rest = rest[0], rest[1:]
    y_ref = rest[0]
    mix = mix_ref[...]
    if mla_heads:
        rank = mix.shape[1] // mla_heads
        mix = jnp.concatenate(
            [_mm(mix[:, h * rank:(h + 1) * rank], wuv_ref[h]) for h in range(mla_heads)], axis=1)
    x2 = x_ref[...] + _mm(mix, wo_ref[...])
    hn = _rms(x2, nf_ref[...]).astype(BF16)
    acc = x2
    dff = up_ref.shape[1]
    fc = min(FF_CHUNK, dff)
    for cidx in range(dff // fc):
        hid = jnp.dot(hn, up_ref[:, cidx * fc:(cidx + 1) * fc], preferred_element_type=F32)
        hid = jnp.square(jnp.maximum(hid, 0.0)).astype(BF16)
        acc = acc + jnp.dot(hid, down_ref[cidx * fc:(cidx + 1) * fc, :], preferred_element_type=F32)
    if final_norm:
        acc = _rms(acc, fn_ref[...])
    y_ref[...] = acc


def _post(mix, x, wo, nf, up, down, wuv=None, fn=None):
    n, d = x.shape
    tm = min(POST_ROW_TILE, n)
    assert n % tm == 0
    ins = [mix, x]
    specs = [pl.BlockSpec((tm, mix.shape[1]), lambda i: (i, 0)), pl.BlockSpec((tm, d), lambda i: (i, 0))]

    def add_const(arr):
        ins.append(arr)
        specs.append(pl.BlockSpec(arr.shape, lambda i, _nd=arr.ndim: (0,) * _nd,
                                  pipeline_mode=pl.Buffered(1)))

    if wuv is not None:
        add_const(wuv)
    for arr in (wo, nf.reshape(1, d).astype(F32), up, down):
        add_const(arr)
    if fn is not None:
        add_const(fn.reshape(1, d).astype(F32))
    return pl.pallas_call(
        functools.partial(_post_kernel, 0 if wuv is None else wuv.shape[0], fn is not None),
        grid=(n // tm,),
        in_specs=specs,
        out_specs=pl.BlockSpec((tm, d), lambda i: (i, 0)),
        out_shape=jax.ShapeDtypeStruct((n, d), F32),
        compiler_params=_params(("parallel",)),
    )(*ins)


def _mla_pre_kernel(heads, scale, x_ref, cos_ref, sin_ref, kvn_ref, wdkv_ref, latn_ref, wkr_ref, wkrp_ref,
                    nmix_ref, wdq_ref, qn_ref, wuqn_ref, wuqr_ref, wuqrp_ref, wukt_ref,
                    lat_ref, kr_ref, kc_ref, qc_ref):
    x = x_ref[...]
    cos = cos_ref[...]
    sin = sin_ref[...]
    rows = x.shape[0]
    rank = wukt_ref.shape[2]
    kw = kc_ref.shape[1]
    pad = kw - rank - QK_ROPE
    kv_in = _rms(x, kvn_ref[...]).astype(BF16)
    lat = _rms(jnp.dot(kv_in, wdkv_ref[...], preferred_element_type=F32), latn_ref[...])
    kr = (jnp.dot(kv_in, wkr_ref[...], preferred_element_type=F32) * cos[:, :QK_ROPE]
          + jnp.dot(kv_in, wkrp_ref[...], preferred_element_type=F32) * sin[:, :QK_ROPE])
    lat_ref[...] = lat
    kr_ref[...] = kr
    kc_ref[:, :rank] = lat.astype(BF16)
    kc_ref[:, rank:rank + QK_ROPE] = kr.astype(BF16)
    if pad:
        kc_ref[:, rank + QK_ROPE:] = jnp.zeros((rows, pad), BF16)
    xn = _rms(x, nmix_ref[...]).astype(BF16)
    cq = _rms(jnp.dot(xn, wdq_ref[...], preferred_element_type=F32), qn_ref[...]).astype(BF16)
    qn = jnp.dot(cq, wuqn_ref[...], preferred_element_type=F32)
    qr = ((jnp.dot(cq, wuqr_ref[...], preferred_element_type=F32) * cos
           + jnp.dot(cq, wuqrp_ref[...], preferred_element_type=F32) * sin) * scale).astype(BF16)
    for h in range(heads):
        ql = _mm(qn[:, h * QK_NOPE:(h + 1) * QK_NOPE], wukt_ref[h])
        qc_ref[:, h * kw:h * kw + rank] = (ql * scale).astype(BF16)
        qc_ref[:, h * kw + rank:h * kw + rank + QK_ROPE] = qr[:, h * QK_ROPE:(h + 1) * QK_ROPE]
        if pad:
            qc_ref[:, h * kw + rank + QK_ROPE:(h + 1) * kw] = jnp.zeros((rows, pad), BF16)


def _rope_tables(pos, heads):
    half = QK_ROPE // 2
    inv_freq = ROPE_BASE ** (-jnp.arange(half, dtype=F32) / half)
    ang = pos.astype(F32)[:, None] * inv_freq[None, :]
    cos, sin = jnp.cos(ang), jnp.sin(ang)
    cos2 = jnp.concatenate([cos, cos], axis=1)
    sin2 = jnp.concatenate([-sin, sin], axis=1)
    return jnp.tile(cos2, (1, heads)), jnp.tile(sin2, (1, heads))


def _swap_halves(wcols):
    half = QK_ROPE // 2
    return jnp.concatenate([wcols[..., half:], wcols[..., :half]], axis=-1)


def _mla_weights(p, j):
    d, rank = p['w_dkv'].shape
    heads = p['w_uk'].shape[1]
    qrank = p['w_dq'].shape[2]
    wuq = p['w_uq'][j]
    wuq_n = wuq[:, :, :QK_NOPE].reshape(qrank, heads * QK_NOPE)
    wuq_r = wuq[:, :, QK_NOPE:]
    row2 = lambda t: t.reshape(1, -1).astype(F32)
    return dict(
        kvn=row2(p['kv_norm']), wdkv=p['w_dkv'].astype(BF16), latn=row2(p['lat_norm']),
        wkr=p['w_kr'].astype(BF16), wkrp=_swap_halves(p['w_kr']).astype(BF16),
        nmix=row2(p['norm_mix'][p['rw_wr'].shape[0] + j]), wdq=p['w_dq'][j].astype(BF16), qn=row2(p['q_norm'][j]),
        wuqn=wuq_n.astype(BF16), wuqr=wuq_r.reshape(qrank, heads * QK_ROPE).astype(BF16),
        wuqrp=_swap_halves(wuq_r).reshape(qrank, heads * QK_ROPE).astype(BF16),
        wukt=jnp.transpose(p['w_uk'], (1, 2, 0)).astype(BF16),
        wuv=jnp.transpose(p['w_uv'], (1, 0, 2)).astype(BF16),
        heads=heads, rank=rank)


def _key_width(rank):
    return -(-(rank + QK_ROPE) // 128) * 128


def _mla_pre(x, cos, sin, mw):
    n, d = x.shape
    heads, rank = mw['heads'], mw['rank']
    kw = _key_width(rank)
    tm = min(ROW_TILE, n)
    assert n % tm == 0 and cos.shape[0] % tm == 0
    tab_blocks = cos.shape[0] // tm
    row = lambda width: pl.BlockSpec((tm, width), lambda i: (i, 0))
    tab = pl.BlockSpec((tm, heads * QK_ROPE), lambda i: (i % tab_blocks, 0))
    names = ('kvn', 'wdkv', 'latn', 'wkr', 'wkrp', 'nmix', 'wdq', 'qn', 'wuqn', 'wuqr', 'wuqrp', 'wukt')
    consts = [mw[k] for k in names]
    scale = float((QK_NOPE + QK_ROPE) ** -0.5)
    return pl.pallas_call(
        functools.partial(_mla_pre_kernel, heads, scale),
        grid=(n // tm,),
        in_specs=[row(d), tab, tab] + [_const_spec(a.shape) for a in consts],
        out_specs=[row(rank), row(QK_ROPE), row(kw), row(heads * kw)],
        out_shape=[jax.ShapeDtypeStruct((n, rank), F32), jax.ShapeDtypeStruct((n, QK_ROPE), F32),
                   jax.ShapeDtypeStruct((n, kw), BF16), jax.ShapeDtypeStruct((n, heads * kw), BF16)],
        compiler_params=_params(("parallel",)),
    )(x, cos, sin, *consts)


def _softmax_update(s, vals, m_scr, l_scr, acc_scr):
    lanes = m_scr.shape[1]
    width = s.shape[1]
    m_prev = m_scr[...]
    m_new = jnp.maximum(m_prev, jnp.max(s, axis=-1, keepdims=True))
    alpha = jnp.exp(m_prev - m_new)
    m_wide = m_new[:, :width] if width <= lanes else jnp.tile(m_new, (1, width // lanes))
    p = jnp.exp(s - m_wide)
    l_scr[...] = alpha * l_scr[...] + jnp.sum(p, axis=-1, keepdims=True)
    acc_scr[...] = (jnp.tile(alpha, (1, acc_scr.shape[1] // lanes)) * acc_scr[...]
                    + jnp.dot(p.astype(BF16), vals, preferred_element_type=F32))
    m_scr[...] = m_new


def _softmax_init(m_scr, l_scr, acc_scr):
    m_scr[...] = jnp.full(m_scr.shape, MASK_VALUE, F32)
    l_scr[...] = jnp.zeros(l_scr.shape, F32)
    acc_scr[...] = jnp.zeros(acc_scr.shape, F32)


def _softmax_result(l_scr, acc_scr):
    return acc_scr[...] / jnp.tile(l_scr[...], (1, acc_scr.shape[1] // l_scr.shape[1]))


def _attn_prompt_kernel(heads, rank, tk, qc_ref, kc_ref, pkc_ref, o_ref, m_scr, l_scr, acc_scr):
    i = pl.program_id(1)
    tq = qc_ref.shape[0]
    kw = kc_ref.shape[2]
    group = heads // ATTN_ROW_PARTS
    rows = group * tq
    qcs = [jnp.concatenate([qc_ref[:, h * kw:(h + 1) * kw] for h in range(g * group, (g + 1) * group)], axis=0)
           for g in range(ATTN_ROW_PARTS)]
    stats = [(m_scr.at[pl.ds(g * rows, rows)], l_scr.at[pl.ds(g * rows, rows)], acc_scr.at[pl.ds(g * rows, rows)])
             for g in range(ATTN_ROW_PARTS)]
    _softmax_init(m_scr, l_scr, acc_scr)
    pkc = pkc_ref[...]
    for qc, st in zip(qcs, stats):
        _softmax_update(_mm_nt(qc, pkc), pkc[:, :rank], *st)
    n_full = (i * tq) // tk

    def body(jb, carry):
        kc = kc_ref[0, pl.ds(pl.multiple_of(jb * tk, tk), tk), :]
        for qc, st in zip(qcs, stats):
            _softmax_update(_mm_nt(qc, kc), kc[:, :rank], *st)
        return carry

    lax.fori_loop(0, n_full, body, 0)
    td = min(ATTN_DIAG_TILE, tk)
    first = n_full * tk
    q_pos = i * tq + lax.broadcasted_iota(jnp.int32, (rows, td), 0) % tq
    k_off = lax.broadcasted_iota(jnp.int32, (rows, td), 1)

    def diag_body(jb, carry):
        start = pl.multiple_of(first + jb * td, td)
        kc = kc_ref[0, pl.ds(start, td), :]
        for qc, st in zip(qcs, stats):
            _softmax_update(jnp.where(k_off + start <= q_pos, _mm_nt(qc, kc), MASK_VALUE), kc[:, :rank], *st)
        return carry

    lax.fori_loop(0, ((i + 1) * tq - first + td - 1) // td, diag_body, 0)
    o = _softmax_result(l_scr, acc_scr).astype(BF16)
    for h in range(heads):
        o_ref[:, h * rank:(h + 1) * rank] = o[h * tq:(h + 1) * tq, :]


def _attn_prompt(qc, kc, pkc, batch, heads, rank):
    n = qc.shape[0]
    t = n // batch
    kw = kc.shape[1]
    tq = min(ATTN_TILE, t)
    tk = min(ATTN_KEY_TILE, t)
    assert t % tk == 0 and tk % tq == 0 and tk % min(ATTN_DIAG_TILE, tk) == 0
    nq = t // tq
    return pl.pallas_call(
        functools.partial(_attn_prompt_kernel, heads, rank, tk),
        grid=(batch, nq),
        in_specs=[pl.BlockSpec((tq, heads * kw), lambda b, i: (b * nq + i, 0)),
                  pl.BlockSpec((1, t, kw), lambda b, i: (b, 0, 0)),
                  _const_spec(pkc.shape)],
        out_specs=pl.BlockSpec((tq, heads * rank), lambda b, i: (b * nq + i, 0)),
        out_shape=jax.ShapeDtypeStruct((n, heads * rank), BF16),
        scratch_shapes=[pltpu.VMEM((heads * tq, 128), F32), pltpu.VMEM((heads * tq, 128), F32),
                        pltpu.VMEM((heads * tq, rank), F32)],
        compiler_params=_params(("parallel", "arbitrary")),
    )(qc, kc.reshape(batch, t, kw), pkc)


def _attn_sample_kernel(chunk_pages, seq, rank, pt_ref, qc_ref, nkc_ref, lat_hbm, krt_hbm, o_ref,
                        lat_buf, krt_buf, sems, m_scr, l_scr, acc_scr):
    b = pl.program_id(0)
    qc = qc_ref[0]
    ahead = SAMPLE_SLOTS - 1
    total_chunks = pl.num_programs(0) * SAMPLE_CHUNKS

    def page_copies(number, p):
        slot = lax.rem(number, SAMPLE_SLOTS)
        page = pt_ref[number // SAMPLE_CHUNKS, lax.rem(number, SAMPLE_CHUNKS) * chunk_pages + p]
        rows = pl.ds(pl.multiple_of(p * PAGE_SIZE, PAGE_SIZE), PAGE_SIZE)
        return (pltpu.make_async_copy(lat_hbm.at[page], lat_buf.at[slot, rows, :], sems.at[slot, 0]),
                pltpu.make_async_copy(krt_hbm.at[page], krt_buf.at[slot, :, rows], sems.at[slot, 1]))

    def start_chunk(number):
        def body(p, carry):
            for cp in page_copies(number, p):
                cp.start()
            return carry
        lax.fori_loop(0, chunk_pages, body, 0, unroll=min(8, chunk_pages))

    def wait_chunk(number):
        def body(p, carry):
            for cp in page_copies(number, p):
                cp.wait()
            return carry
        lax.fori_loop(0, chunk_pages, body, 0, unroll=min(8, chunk_pages))

    @pl.when(b == 0)
    def _():
        for number in range(ahead):
            start_chunk(jnp.int32(number))

    _softmax_init(m_scr, l_scr, acc_scr)
    for chunk in range(SAMPLE_CHUNKS):
        number = b * SAMPLE_CHUNKS + chunk
        wait_chunk(number)

        @pl.when(number + ahead < total_chunks)
        def _():
            start_chunk(number + ahead)

        slot = lax.rem(number, SAMPLE_SLOTS)
        kl = lat_buf[slot].astype(BF16)
        krt = krt_buf[slot].astype(BF16)
        s = _mm_nt(qc[:, :rank], kl) + jnp.dot(qc[:, rank:rank + QK_ROPE], krt, preferred_element_type=F32)
        _softmax_update(s, kl, m_scr, l_scr, acc_scr)

    nkc = nkc_ref[0]
    s = _mm_nt(qc, nkc)
    q_t = lax.broadcasted_iota(jnp.int32, s.shape, 0) % seq
    k_t = lax.broadcasted_iota(jnp.int32, s.shape, 1)
    s = jnp.where(k_t <= q_t, s, MASK_VALUE)
    _softmax_update(s, nkc[:, :rank], m_scr, l_scr, acc_scr)
    o_ref[0] = _softmax_result(l_scr, acc_scr).astype(BF16)


def _attn_sample(qc, nkc, seq, rank, cache_latent, cache_krope_t, page_table):
    b, rows, kw = qc.shape
    total_pages = page_table.shape[1]
    assert total_pages % SAMPLE_CHUNKS == 0 and 2 <= SAMPLE_SLOTS <= b * SAMPLE_CHUNKS
    chunk_pages = total_pages // SAMPLE_CHUNKS
    chunk_keys = chunk_pages * PAGE_SIZE
    per_b = lambda arr: pl.BlockSpec((1,) + arr.shape[1:], lambda bi, pt: (bi, 0, 0))
    grid_spec = pltpu.PrefetchScalarGridSpec(
        num_scalar_prefetch=1,
        grid=(b,),
        in_specs=[per_b(qc), per_b(nkc), pl.BlockSpec(memory_space=pl.ANY), pl.BlockSpec(memory_space=pl.ANY)],
        out_specs=pl.BlockSpec((1, rows, rank), lambda bi, pt: (bi, 0, 0)),
        scratch_shapes=[pltpu.VMEM((SAMPLE_SLOTS, chunk_keys, rank), F32),
                        pltpu.VMEM((SAMPLE_SLOTS, QK_ROPE, chunk_keys), F32),
                        pltpu.SemaphoreType.DMA((SAMPLE_SLOTS, 2)),
                        pltpu.VMEM((rows, 128), F32), pltpu.VMEM((rows, 128), F32), pltpu.VMEM((rows, rank), F32)],
    )
    return pl.pallas_call(
        functools.partial(_attn_sample_kernel, chunk_pages, seq, rank),
        grid_spec=grid_spec,
        out_shape=jax.ShapeDtypeStruct((b, rows, rank), BF16),
        compiler_params=_params(("arbitrary",)),
    )(page_table, qc, nkc, cache_latent, cache_krope_t)


def _layer0(x, shift0, wkv0, p, flat):
    b, t, d = x.shape
    weights = _rwkv_pre_weights(p, 0)
    pre = _rwkv_pre_flat if flat else _rwkv_pre_seq
    seven, shift_new = pre(x, shift0, p['norm_mix'][0], weights)
    scan = _wkv_lanes if flat else _wkv_scan
    og, wkv_new = scan(seven, p['rw_rk'][0], p['rw_lnx_g'][0], p['rw_lnx_b'][0], wkv0)
    x1 = _post(og.reshape(b * t, d), x.reshape(b * t, d), p['rw_wo'][0].astype(BF16), p['norm_ffn'][0],
               p['ffn_up'][0].astype(BF16), p['ffn_down'][0].astype(BF16))
    return x1, wkv_new, shift_new


def _layer1_post(olat, x1, p, mw):
    return _post(olat, x1, p['w_o_mla'][0].astype(BF16), p['norm_ffn'][1], p['ffn_up'][1].astype(BF16),
                 p['ffn_down'][1].astype(BF16), wuv=mw['wuv'], fn=p['norm_final'])


def kernel(x_prompt, x_sample, state_wkv, state_shift, cache_latent, cache_krope, page_table,
           meta_tokens, rw_mu, rw_wr, rw_wk, rw_wv, rw_wo, rw_w0, rw_w1, rw_w2, rw_a0, rw_a1, rw_a2,
           rw_g1, rw_g2, rw_kk, rw_ka, rw_rk, rw_lnx_g, rw_lnx_b, norm_mix, norm_ffn, ffn_up, ffn_down,
           kv_norm, w_dkv, lat_norm, w_kr, w_uk, w_uv, w_dq, q_norm, w_uq, w_o_mla, norm_final):
    p = dict(rw_mu=rw_mu, rw_wr=rw_wr, rw_wk=rw_wk, rw_wv=rw_wv, rw_wo=rw_wo, rw_w0=rw_w0, rw_w1=rw_w1,
             rw_w2=rw_w2, rw_a0=rw_a0, rw_a1=rw_a1, rw_a2=rw_a2, rw_g1=rw_g1, rw_g2=rw_g2, rw_kk=rw_kk,
             rw_ka=rw_ka, rw_rk=rw_rk, rw_lnx_g=rw_lnx_g, rw_lnx_b=rw_lnx_b, norm_mix=norm_mix,
             norm_ffn=norm_ffn, ffn_up=ffn_up, ffn_down=ffn_down, kv_norm=kv_norm, w_dkv=w_dkv,
             lat_norm=lat_norm, w_kr=w_kr, w_uk=w_uk, w_uv=w_uv, w_dq=w_dq, q_norm=q_norm, w_uq=w_uq,
             w_o_mla=w_o_mla, norm_final=norm_final)
    assert rw_wr.shape[0] == 1 and w_dq.shape[0] == 1, "one RWKV layer followed by one MLA layer"
    bsz, seq, d = x_prompt.shape
    dbsz, dseq, _ = x_sample.shape
    rw_heads = d // RW_HEAD
    mw = _mla_weights(p, 0)
    heads, rank = mw['heads'], mw['rank']
    n_meta = meta_tokens.shape[0]
    past_len = page_table.shape[1] * PAGE_SIZE

    xm1, wkv_m, shift_m = _layer0(meta_tokens[None].astype(F32), jnp.zeros((1, d), F32),
                                  jnp.zeros((1, rw_heads, RW_HEAD, RW_HEAD), F32), p, flat=False)
    cos_m, sin_m = _rope_tables(jnp.arange(n_meta, dtype=jnp.int32), heads)
    lat_m, kr_m, kc_m, _ = _mla_pre(xm1, cos_m, sin_m, mw)

    xp1, wkv_p, shift_p = _layer0(x_prompt, jnp.broadcast_to(shift_m, (bsz, d)),
                                  jnp.broadcast_to(wkv_m, (bsz,) + wkv_m.shape[1:]), p, flat=False)
    cos_p, sin_p = _rope_tables(n_meta + jnp.arange(seq, dtype=jnp.int32), heads)
    lat_p, kr_p, kc_p, qc_p = _mla_pre(xp1, cos_p, sin_p, mw)
    olat_p = _attn_prompt(qc_p, kc_p, kc_m, bsz, heads, rank)
    y_prompt = _layer1_post(olat_p, xp1, p, mw).reshape(bsz, seq, d)
    latent_prompt = jnp.concatenate(
        [jnp.broadcast_to(lat_m[None], (bsz, n_meta, rank)), lat_p.reshape(bsz, seq, rank)], axis=1)
    krope_prompt = jnp.concatenate(
        [jnp.broadcast_to(kr_m[None], (bsz, n_meta, QK_ROPE)), kr_p.reshape(bsz, seq, QK_ROPE)], axis=1)

    xs1, wkv_s, shift_s = _layer0(x_sample, state_shift[0], jnp.transpose(state_wkv[0], (1, 2, 3, 0)), p, flat=True)
    wkv_s = jnp.transpose(wkv_s, (3, 0, 1, 2))
    cos_s, sin_s = _rope_tables(past_len + jnp.arange(dseq, dtype=jnp.int32), heads)
    cos_s, sin_s = jnp.tile(cos_s, (dbsz, 1)), jnp.tile(sin_s, (dbsz, 1))
    lat_s, kr_s, kc_s, qc_s = _mla_pre(xs1, cos_s, sin_s, mw)
    kw = kc_s.shape[1]
    qc_s = jnp.transpose(qc_s.reshape(dbsz, dseq, heads, kw), (0, 2, 1, 3)).reshape(dbsz, heads * dseq, kw)
    new_rows = -(-dseq // 16) * 16
    nkc_s = jnp.pad(kc_s.reshape(dbsz, dseq, kw), ((0, 0), (0, new_rows - dseq), (0, 0)))
    olat_s = _attn_sample(qc_s, nkc_s, dseq, rank, cache_latent, jnp.swapaxes(cache_krope, 1, 2), page_table)
    olat_s = jnp.transpose(olat_s.reshape(dbsz, heads, dseq, rank), (0, 2, 1, 3)).reshape(dbsz * dseq, heads * rank)
    y_sample = _layer1_post(olat_s, xs1, p, mw).reshape(dbsz, dseq, d)

    return (y_prompt, y_sample, wkv_p[None], shift_p[None], latent_prompt, krope_prompt,
            wkv_s[None], shift_s[None], lat_s.reshape(dbsz, dseq, rank), kr_s.reshape(dbsz, dseq, QK_ROPE))
```

```python
import functools
import math

import jax
import jax.numpy as jnp
from jax import lax
from jax.experimental import pallas as pl
from jax.experimental.pallas import tpu as pltpu

F32 = jnp.float32
BF16 = jnp.bfloat16

RW_HEAD = 64
QK_NOPE = 128
QK_ROPE = 64
V_HEAD = 128
N_META = 16
PAGE_SIZE = 128
NORM_EPS = 1e-6
GN_EPS = 64e-5
ROPE_BASE = 10000.0
DECAY_SCALE = math.exp(-0.5)
MASK_VALUE = -1e30

V7X_VMEM_LIMIT_BYTES = 56 * 1024 * 1024
SCAN_CHUNK = 64
SCAN_MIN_CHUNK = 16
SCAN_SEQS_PER_STEP = 2
ROW_TILE = 512
POST_ROW_TILE = 512
FF_CHUNK = 1024
ATTN_TILE = 256
ATTN_KEY_TILE = 1024
ATTN_DIAG_TILE = 512
ATTN_ROW_PARTS = 4
SAMPLE_CHUNKS = 2
SAMPLE_SLOTS = 3


def _mm(a, b):
    return jnp.dot(a.astype(BF16), b.astype(BF16), preferred_element_type=F32)


def _mm_nt(a, b):
    return lax.dot_general(a.astype(BF16), b.astype(BF16), (((1,), (1,)), ((), ())),
                           preferred_element_type=F32)


def _split3(x):
    h = x.astype(BF16)
    r1 = x - h.astype(F32)
    m = r1.astype(BF16)
    lo = (r1 - m.astype(F32)).astype(BF16)
    return h, m, lo


def _mm_exact_lhs(sel, x):
    h, m, lo = _split3(x)
    sel = sel.astype(BF16)
    dot = lambda t: jnp.dot(sel, t, preferred_element_type=F32)
    return dot(h) + (dot(m) + dot(lo))


def _transpose_f32(x):
    n = x.shape[1]
    eye = (lax.broadcasted_iota(jnp.int32, (n, n), 0) == lax.broadcasted_iota(jnp.int32, (n, n), 1))
    eye = eye.astype(BF16)
    h, m, lo = _split3(x)
    dot = lambda t: lax.dot_general(eye, t, (((1,), (1,)), ((), ())), preferred_element_type=F32)
    return dot(h) + (dot(m) + dot(lo))


def _rms(x, g):
    return x * lax.rsqrt(jnp.mean(x * x, axis=-1, keepdims=True) + NORM_EPS) * g


def _sigmoid(z):
    return 1.0 / (1.0 + jnp.exp(-z))


def _const_spec(shape):
    nd = len(shape)
    return pl.BlockSpec(shape, lambda *_: (0,) * nd)


def _params(sem):
    return pltpu.CompilerParams(dimension_semantics=sem, vmem_limit_bytes=V7X_VMEM_LIMIT_BYTES)


def _rwkv_pre_math(xn, prev, w, outs):
    (mu, wr, wk, wv, w0, w1, w2, a0, a1, a2, g1, g2, kkp, kap) = w
    r_ref, w_ref, k_ref, v_ref, kk_ref, a_ref, g_ref = outs
    xx = prev - xn
    mix = lambda m: (xn + xx * mu[m:m + 1, :]).astype(BF16)
    r = _mm(mix(0), wr[...])
    z = w0[...] + _mm(jnp.tanh(_mm(mix(1), w1[...])), w2[...])
    wdec = -DECAY_SCALE * _sigmoid(z)
    k = _mm(mix(2), wk[...])
    v = _mm(mix(3), wv[...])
    a = _sigmoid(a0[...] + _mm(_mm(mix(4), a1[...]), a2[...]))
    g = _mm(_sigmoid(_mm(mix(5), g1[...])), g2[...])
    r_ref[...] = r.reshape(r_ref.shape)
    w_ref[...] = wdec.reshape(w_ref.shape)
    k_ref[...] = (k * (1.0 + (a - 1.0) * kap[...])).reshape(k_ref.shape)
    v_ref[...] = v.reshape(v_ref.shape)
    kk_ref[...] = (k * kkp[...]).reshape(kk_ref.shape)
    a_ref[...] = a.reshape(a_ref.shape)
    g_ref[...] = g.reshape(g_ref.shape)


def _rwkv_pre_seq_kernel(x_ref, halo_ref, s0_ref, gmix_ref, *rest):
    w, outs, tail_ref = rest[:14], rest[14:21], rest[21]
    j = pl.program_id(1)
    x = x_ref[0]
    tm = x.shape[0]
    gm = gmix_ref[...]
    xn = _rms(x, gm)
    row = lax.broadcasted_iota(jnp.int32, (tm, 1), 0)
    prev_raw = jnp.where(row == 0, halo_ref[0][7:8, :], pltpu.roll(x, 1, 0))
    prev = jnp.where((row == 0) & (j == 0), s0_ref[0], _rms(prev_raw, gm))
    tail_ref[0] = xn[tm - 8:, :]
    _rwkv_pre_math(xn, prev, w, outs)


def _rwkv_pre_flat_kernel(seq_len, x_ref, xs_ref, s0x_ref, gmix_ref, *rest):
    w, outs, xn_ref = rest[:14], rest[14:21], rest[21]
    x = x_ref[...]
    tm = x.shape[0]
    gm = gmix_ref[...]
    xn = _rms(x, gm)
    row = lax.broadcasted_iota(jnp.int32, (tm, 1), 0)
    prev = jnp.where(row % seq_len == 0, s0x_ref[...], _rms(xs_ref[...], gm))
    xn_ref[...] = xn
    _rwkv_pre_math(xn, prev, w, outs)


def _rwkv_pre_weights(p, i):
    row2 = lambda t: t.reshape(1, -1).astype(F32)
    return (p['rw_mu'][i].astype(F32), p['rw_wr'][i].astype(BF16), p['rw_wk'][i].astype(BF16),
            p['rw_wv'][i].astype(BF16), row2(p['rw_w0'][i]), p['rw_w1'][i].astype(BF16),
            p['rw_w2'][i].astype(BF16), row2(p['rw_a0'][i]), p['rw_a1'][i].astype(BF16),
            p['rw_a2'][i].astype(BF16), p['rw_g1'][i].astype(BF16), p['rw_g2'][i].astype(BF16),
            row2(p['rw_kk'][i]), row2(p['rw_ka'][i]))


def _rwkv_pre_seq(x, shift0, gmix, weights):
    b, t, d = x.shape
    tm = min(ROW_TILE, t)
    assert t % tm == 0 and tm % 8 == 0
    blk = pl.BlockSpec((1, tm, d), lambda bi, j: (bi, j, 0))
    halo = pl.BlockSpec((1, 8, d), lambda bi, j: (bi, jnp.maximum(j * (tm // 8) - 1, 0), 0))
    s0 = pl.BlockSpec((1, 1, d), lambda bi, j: (bi, 0, 0))
    tail = pl.BlockSpec((1, 8, d), lambda bi, j: (bi, 0, 0))
    out = pl.pallas_call(
        _rwkv_pre_seq_kernel,
        grid=(b, t // tm),
        in_specs=[blk, halo, s0, _const_spec((1, d))] + [_const_spec(w.shape) for w in weights],
        out_specs=[blk] * 7 + [tail],
        out_shape=[jax.ShapeDtypeStruct((b, t, d), F32)] * 7 + [jax.ShapeDtypeStruct((b, 8, d), F32)],
        compiler_params=_params(("parallel", "arbitrary")),
    )(x, x, shift0.reshape(b, 1, d), gmix.reshape(1, d), *weights)
    return out[:7], out[7][:, 7, :]


def _rwkv_pre_flat(x, shift0, gmix, weights):
    b, t, d = x.shape
    n = b * t
    tm = min(ROW_TILE, n)
    assert n % tm == 0 and tm % t == 0
    xf = x.reshape(n, d)
    xs = jnp.concatenate([jnp.zeros((1, d), F32), xf[:-1]], axis=0)
    s0x = jnp.concatenate([shift0[:, None, :], jnp.zeros((b, t - 1, d), F32)], axis=1).reshape(n, d)
    blk = pl.BlockSpec((tm, d), lambda i: (i, 0))
    out = pl.pallas_call(
        functools.partial(_rwkv_pre_flat_kernel, t),
        grid=(n // tm,),
        in_specs=[blk, blk, blk, _const_spec((1, d))] + [_const_spec(w.shape) for w in weights],
        out_specs=[blk] * 8,
        out_shape=[jax.ShapeDtypeStruct((n, d), F32)] * 8,
        compiler_params=_params(("parallel",)),
    )(xf, xs, s0x, gmix.reshape(1, d), *weights)
    seven = [o.reshape(b, t, d) for o in out[:7]]
    return seven, out[7].reshape(b, t, d)[:, t - 1, :]


def _wkv_chunk_kernel(nb, tc, c, heads, r_ref, w_ref, k_ref, v_ref, kk_ref, a_ref, g_ref,
                      rk_ref, lng_ref, lnb_ref, s0_ref, og_ref, sout_ref, h_scr):
    step = pl.program_id(1)
    n = RW_HEAD
    pw_lanes = 2 * n
    npairs = heads // 2
    pairs = [(e, p) for e in range(nb) for p in range(npairs)]
    items = [(e, p, hh) for e, p in pairs for hh in range(2)]
    zeros_nn = jnp.zeros((n, n), F32)

    @pl.when(step == 0)
    def _():
        for i, (e, p) in enumerate(pairs):
            t0 = _transpose_f32(s0_ref[e, 2 * p])
            t1 = _transpose_f32(s0_ref[e, 2 * p + 1])
            h_scr[i] = jnp.concatenate([jnp.concatenate([t0, zeros_nn], axis=1),
                                        jnp.concatenate([zeros_nn, t1], axis=1)], axis=0)

    iota = lambda shape, dim: lax.broadcasted_iota(jnp.int32, shape, dim)
    tril_incl = iota((c, c), 0) >= iota((c, c), 1)
    tril_strict = iota((c, c), 0) > iota((c, c), 1)
    eye_c = (iota((c, c), 0) == iota((c, c), 1)).astype(F32)
    uk_mask = (iota((c, 2 * c), 1) >= c) & (iota((c, 2 * c), 0) > iota((c, 2 * c), 1) - c)
    ar_mask = iota((c, 2 * c), 0) >= iota((c, 2 * c), 1) % c
    head_of = lambda rows: iota((rows, pw_lanes), 1) >= n
    second_wide = lambda rows: iota((rows, 2 * pw_lanes), 1) % pw_lanes >= n
    eye_pair = iota((pw_lanes, pw_lanes), 0) == iota((pw_lanes, pw_lanes), 1)
    pick = [iota((n, pw_lanes), 1) == iota((n, pw_lanes), 0) + n * hh for hh in range(2)]
    zeros_c = jnp.zeros((c, pw_lanes), BF16)

    def half_sums(x):
        second = head_of(c)
        return jnp.where(second, jnp.sum(jnp.where(second, x, 0.0), axis=-1, keepdims=True),
                         jnp.sum(jnp.where(second, 0.0, x), axis=-1, keepdims=True))

    def token_terms(e):
        def load(ref):
            x = ref[e]
            if tc < c:
                x = jnp.concatenate([x, jnp.zeros((c - tc, x.shape[1]), F32)], axis=0)
            return x

        r, w, k, v, kk, a, g = (load(t) for t in (r_ref, w_ref, k_ref, v_ref, kk_ref, a_ref, g_ref))
        kk = jnp.concatenate(
            [x * jnp.minimum(lax.rsqrt(half_sums(x * x)), 1e12)
             for x in (kk[:, p * pw_lanes:(p + 1) * pw_lanes] for p in range(npairs))], axis=1)
        lw = _mm_exact_lhs(tril_incl, w)
        lw_end = lw[c - 1:c, :]
        p_inv = jnp.exp(-lw)
        p_tail = jnp.exp(lw_end - lw)
        kb = kk * a
        rt = r * jnp.exp(lw)
        return dict(
            v=v, g=g, rt=rt, p_end=jnp.exp(lw_end), bonus_rk=r * k * rk_ref[...],
            kt=(kk * jnp.exp(lw - w)).astype(BF16), rt_b=rt.astype(BF16), nbt=(-(kb * p_inv)).astype(BF16),
            kti=(k * p_inv).astype(BF16), nbh=(-(kb * p_tail)).astype(BF16), kh=(k * p_tail).astype(BF16),
            vb=v.astype(BF16))

    terms = [token_terms(e) for e in range(nb)]
    col = lambda name, e, p: terms[e][name][:, p * pw_lanes:(p + 1) * pw_lanes]
    of_pair = lambda xs, e, p: [x for x, it in zip(xs, items) if it[:2] == (e, p)]
    by_head = lambda rows, x0, x1: jnp.where(head_of(rows), x1, x0)

    own = lambda rows, hh: head_of(rows) if hh else jnp.logical_not(head_of(rows))
    aa = [_mm_nt(jnp.where(own(2 * c, hh), jnp.concatenate([col('kt', e, p), col('rt_b', e, p)], axis=0), 0.0),
                 jnp.concatenate([col('nbt', e, p), col('kti', e, p)], axis=0))
          for e, p, hh in items]
    a_ub = [jnp.where(tril_strict, x[:c, :c], 0.0) for x in aa]
    a_uk = [jnp.where(uk_mask, x[:c, :], 0.0).astype(BF16) for x in aa]
    a_r = [jnp.where(ar_mask, x[c:, :], 0.0).astype(BF16) for x in aa]
    tinv = [eye_c + x for x in a_ub]
    pw = [x.astype(BF16) for x in a_ub]
    pw = [jnp.dot(x, x, preferred_element_type=F32).astype(BF16) for x in pw]
    span = 2
    while 2 * span < c:
        prod = [jnp.dot(x, jnp.concatenate([t.astype(BF16), x], axis=1), preferred_element_type=F32)
                for x, t in zip(pw, tinv)]
        tinv = [t + y[:, :c] for t, y in zip(tinv, prod)]
        pw = [y[:, c:].astype(BF16) for y in prod]
        span *= 2
    tinv = [t + jnp.dot(x, t.astype(BF16), preferred_element_type=F32) for x, t in zip(pw, tinv)]
    av = [jnp.dot(x, jnp.concatenate([zeros_c, col('vb', e, p)], axis=0), preferred_element_type=F32)
          for x, (e, p, hh) in zip(a_uk, items)]
    wu = [_mm(t, jnp.concatenate([col('kt', e, p), x.astype(BF16)], axis=1))
          for t, x, (e, p, hh) in zip(tinv, av, items)]
    zt = [_mm_nt(pick[hh], jnp.concatenate([col('nbh', e, p), col('kh', e, p)], axis=0))
          for e, p, hh in items]
    rhs2 = []
    for e, p in pairs:
        w0, w1 = of_pair(wu, e, p)
        w_pair = by_head(c, w0[:, :pw_lanes], w1[:, :pw_lanes]).astype(BF16)
        u_pair = by_head(c, w0[:, pw_lanes:], w1[:, pw_lanes:]).astype(BF16)
        rhs2.append(jnp.concatenate([jnp.concatenate([w_pair, zeros_c], axis=0),
                                     jnp.concatenate([u_pair, col('vb', e, p)], axis=0)], axis=1))
    res = [_mm(jnp.concatenate([x, z.astype(BF16)], axis=0), rhs2[pairs.index((e, p))])
           for x, z, (e, p, hh) in zip(a_r, zt, items)]
    for i, (e, p) in enumerate(pairs):
        r0, r1 = of_pair(res, e, p)
        top = jnp.where(second_wide(c), r1[:c], r0[:c])
        low = jnp.concatenate([jnp.where(second_wide(n), 0.0, r0[c:]),
                               jnp.where(second_wide(n), r1[c:], 0.0)], axis=0)
        m_bd = jnp.where(eye_pair, col('p_end', e, p), 0.0) + low[:, :pw_lanes]
        ro = _mm(jnp.concatenate([col('rt', e, p) + top[:, :pw_lanes], m_bd], axis=0), h_scr[i])
        o = ro[:c] + top[:, pw_lanes:]
        h_scr[i] = ro[c:] + low[:, pw_lanes:]
        mean = half_sums(o) * (1.0 / n)
        var = half_sums(jnp.square(o - mean)) * (1.0 / n)
        on = (o - mean) * lax.rsqrt(var + GN_EPS)
        bonus = half_sums(col('bonus_rk', e, p)) * col('v', e, p)
        lanes = slice(p * pw_lanes, (p + 1) * pw_lanes)
        out = (on * lng_ref[:, lanes] + lnb_ref[:, lanes] + bonus) * col('g', e, p)
        og_ref[e, :, lanes] = out[:tc]

    @pl.when(step == pl.num_programs(1) - 1)
    def _():
        for i, (e, p) in enumerate(pairs):
            hbd = h_scr[i]
            sout_ref[e, 2 * p] = _transpose_f32(hbd[:n, :n])
            sout_ref[e, 2 * p + 1] = _transpose_f32(hbd[n:, n:])


def _wkv_scan(seven, rk, lng, lnb, state0):
    b, t, d = seven[0].shape
    heads = d // RW_HEAD
    tc = min(SCAN_CHUNK, t)
    c = max(tc, SCAN_MIN_CHUNK)
    nb = SCAN_SEQS_PER_STEP if b % SCAN_SEQS_PER_STEP == 0 else 1
    assert t % tc == 0 and heads % 2 == 0
    blk = pl.BlockSpec((nb, tc, d), lambda bi, j: (bi, j, 0))
    st = pl.BlockSpec((nb, heads, RW_HEAD, RW_HEAD), lambda bi, j: (bi, 0, 0, 0))
    og, s_out = pl.pallas_call(
        functools.partial(_wkv_chunk_kernel, nb, tc, c, heads),
        grid=(b // nb, t // tc),
        in_specs=[blk] * 7 + [_const_spec((1, d))] * 3 + [st],
        out_specs=[blk, st],
        out_shape=[jax.ShapeDtypeStruct((b, t, d), F32),
                   jax.ShapeDtypeStruct((b, heads, RW_HEAD, RW_HEAD), F32)],
        scratch_shapes=[pltpu.VMEM((nb * (heads // 2), 2 * RW_HEAD, 2 * RW_HEAD), F32)],
        compiler_params=_params(("parallel", "arbitrary")),
    )(*seven, rk.reshape(1, d).astype(F32), lng.reshape(1, d).astype(F32),
      lnb.reshape(1, d).astype(F32), state0)
    return og, s_out


def _wkv_lanes_kernel(steps, r_ref, w_ref, k_ref, v_ref, kk_ref, a_ref, g_ref, rk_ref, lng_ref, lnb_ref,
                      s0_ref, og_ref, sout_ref):
    s = s0_ref[0]
    for t in range(steps):
        kk = kk_ref[t]
        kk = kk * jnp.minimum(lax.rsqrt(jnp.sum(kk * kk, axis=0, keepdims=True)), 1e12)
        k = k_ref[t]
        v = v_ref[t]
        r = r_ref[t]
        sa = jnp.sum(s * kk[None], axis=1)
        s = s * jnp.exp(w_ref[t])[None] - sa[:, None, :] * (kk * a_ref[t])[None] + v[:, None, :] * k[None]
        o = jnp.sum(s * r[None], axis=1)
        mean = jnp.mean(o, axis=0, keepdims=True)
        var = jnp.mean(jnp.square(o - mean), axis=0, keepdims=True)
        on = (o - mean) * lax.rsqrt(var + GN_EPS)
        bonus = jnp.sum(r * k * rk_ref[...], axis=0, keepdims=True) * v
        og_ref[t] = (on * lng_ref[...] + lnb_ref[...] + bonus) * g_ref[t]
    sout_ref[0] = s


def _wkv_lanes(seven, rk, lng, lnb, state0_t):
    b, t, d = seven[0].shape
    heads = d // RW_HEAD
    tok = [jnp.transpose(x, (1, 2, 0)) for x in seven]
    col = lambda p: jnp.broadcast_to(p.reshape(d, 1).astype(F32), (d, b))
    blk = pl.BlockSpec((t, RW_HEAD, b), lambda h: (0, h, 0))
    par = pl.BlockSpec((RW_HEAD, b), lambda h: (h, 0))
    st = pl.BlockSpec((1, RW_HEAD, RW_HEAD, b), lambda h: (h, 0, 0, 0))
    og, s_out = pl.pallas_call(
        functools.partial(_wkv_lanes_kernel, t),
        grid=(heads,),
        in_specs=[blk] * 7 + [par] * 3 + [st],
        out_specs=[blk, st],
        out_shape=[jax.ShapeDtypeStruct((t, d, b), F32), jax.ShapeDtypeStruct(state0_t.shape, F32)],
        compiler_params=_params(("parallel",)),
    )(*tok, col(rk), col(lng), col(lnb), state0_t)
    return jnp.transpose(og, (2, 0, 1)), s_out


def _post_kernel(mla_heads, final_norm, mix_ref, x_ref, *rest):
    if mla_heads:
        wuv_ref, rest = rest[0], rest[1:]
    wo_ref, nf_ref, up_ref, down_ref = rest[:4]
    rest = rest[4:]
    if final_norm:
        fn_ref, rest = rest[0], rest[1:]
    y_ref = rest[0]
    mix = mix_ref[...]
    if mla_heads:
        rank = mix.shape[1] // mla_heads
        mix = jnp.concatenate(
            [_mm(mix[:, h * rank:(h + 1) * rank], wuv_ref[h]) for h in range(mla_heads)], axis=1)
    x2 = x_ref[...] + _mm(mix, wo_ref[...])
    hn = _rms(x2, nf_ref[...]).astype(BF16)
    acc = x2
    dff = up_ref.shape[1]
    fc = min(FF_CHUNK, dff)
    for cidx in range(dff // fc):
        hid = jnp.dot(hn, up_ref[:, cidx * fc:(cidx + 1) * fc], preferred_element_type=F32)
        hid = jnp.square(jnp.maximum(hid, 0.0)).astype(BF16)
        acc = acc + jnp.dot(hid, down_ref[cidx * fc:(cidx + 1) * fc, :], preferred_element_type=F32)
    if final_norm:
        acc = _rms(acc, fn_ref[...])
    y_ref[...] = acc


def _post(mix, x, wo, nf, up, down, wuv=None, fn=None):
    n, d = x.shape
    tm = min(POST_ROW_TILE, n)
    assert n % tm == 0
    ins = [mix, x]
    specs = [pl.BlockSpec((tm, mix.shape[1]), lambda i: (i, 0)), pl.BlockSpec((tm, d), lambda i: (i, 0))]

    def add_const(arr):
        ins.append(arr)
        specs.append(pl.BlockSpec(arr.shape, lambda i, _nd=arr.ndim: (0,) * _nd,
                                  pipeline_mode=pl.Buffered(1)))

    if wuv is not None:
        add_const(wuv)
    for arr in (wo, nf.reshape(1, d).astype(F32), up, down):
        add_const(arr)
    if fn is not None:
        add_const(fn.reshape(1, d).astype(F32))
    return pl.pallas_call(
        functools.partial(_post_kernel, 0 if wuv is None else wuv.shape[0], fn is not None),
        grid=(n // tm,),
        in_specs=specs,
        out_specs=pl.BlockSpec((tm, d), lambda i: (i, 0)),
        out_shape=jax.ShapeDtypeStruct((n, d), F32),
        compiler_params=_params(("parallel",)),
    )(*ins)


def _mla_pre_kernel(heads, scale, x_ref, cos_ref, sin_ref, kvn_ref, wdkv_ref, latn_ref, wkr_ref, wkrp_ref,
                    nmix_ref, wdq_ref, qn_ref, wuqn_ref, wuqr_ref, wuqrp_ref, wukt_ref,
                    lat_ref, kr_ref, kc_ref, qc_ref):
    x = x_ref[...]
    cos = cos_ref[...]
    sin = sin_ref[...]
    rows = x.shape[0]
    rank = wukt_ref.shape[2]
    kw = kc_ref.shape[1]
    pad = kw - rank - QK_ROPE
    kv_in = _rms(x, kvn_ref[...]).astype(BF16)
    lat = _rms(jnp.dot(kv_in, wdkv_ref[...], preferred_element_type=F32), latn_ref[...])
    kr = (jnp.dot(kv_in, wkr_ref[...], preferred_element_type=F32) * cos[:, :QK_ROPE]
          + jnp.dot(kv_in, wkrp_ref[...], preferred_element_type=F32) * sin[:, :QK_ROPE])
    lat_ref[...] = lat
    kr_ref[...] = kr
    kc_ref[:, :rank] = lat.astype(BF16)
    kc_ref[:, rank:rank + QK_ROPE] = kr.astype(BF16)
    if pad:
        kc_ref[:, rank + QK_ROPE:] = jnp.zeros((rows, pad), BF16)
    xn = _rms(x, nmix_ref[...]).astype(BF16)
    cq = _rms(jnp.dot(xn, wdq_ref[...], preferred_element_type=F32), qn_ref[...]).astype(BF16)
    qn = jnp.dot(cq, wuqn_ref[...], preferred_element_type=F32)
    qr = ((jnp.dot(cq, wuqr_ref[...], preferred_element_type=F32) * cos
           + jnp.dot(cq, wuqrp_ref[...], preferred_element_type=F32) * sin) * scale).astype(BF16)
    for h in range(heads):
        ql = _mm(qn[:, h * QK_NOPE:(h + 1) * QK_NOPE], wukt_ref[h])
        qc_ref[:, h * kw:h * kw + rank] = (ql * scale).astype(BF16)
        qc_ref[:, h * kw + rank:h * kw + rank + QK_ROPE] = qr[:, h * QK_ROPE:(h + 1) * QK_ROPE]
        if pad:
            qc_ref[:, h * kw + rank + QK_ROPE:(h + 1) * kw] = jnp.zeros((rows, pad), BF16)


def _rope_tables(pos, heads):
    half = QK_ROPE // 2
    inv_freq = ROPE_BASE ** (-jnp.arange(half, dtype=F32) / half)
    ang = pos.astype(F32)[:, None] * inv_freq[None, :]
    cos, sin = jnp.cos(ang), jnp.sin(ang)
    cos2 = jnp.concatenate([cos, cos], axis=1)
    sin2 = jnp.concatenate([-sin, sin], axis=1)
    return jnp.tile(cos2, (1, heads)), jnp.tile(sin2, (1, heads))


def _swap_halves(wcols):
    half = QK_ROPE // 2
    return jnp.concatenate([wcols[..., half:], wcols[..., :half]], axis=-1)


def _mla_weights(p, j):
    d, rank = p['w_dkv'].shape
    heads = p['w_uk'].shape[1]
    qrank = p['w_dq'].shape[2]
    wuq = p['w_uq'][j]
    wuq_n = wuq[:, :, :QK_NOPE].reshape(qrank, heads * QK_NOPE)
    wuq_r = wuq[:, :, QK_NOPE:]
    row2 = lambda t: t.reshape(1, -1).astype(F32)
    return dict(
        kvn=row2(p['kv_norm']), wdkv=p['w_dkv'].astype(BF16), latn=row2(p['lat_norm']),
        wkr=p['w_kr'].astype(BF16), wkrp=_swap_halves(p['w_kr']).astype(BF16),
        nmix=row2(p['norm_mix'][p['rw_wr'].shape[0] + j]), wdq=p['w_dq'][j].astype(BF16), qn=row2(p['q_norm'][j]),
        wuqn=wuq_n.astype(BF16), wuqr=wuq_r.reshape(qrank, heads * QK_ROPE).astype(BF16),
        wuqrp=_swap_halves(wuq_r).reshape(qrank, heads * QK_ROPE).astype(BF16),
        wukt=jnp.transpose(p['w_uk'], (1, 2, 0)).astype(BF16),
        wuv=jnp.transpose(p['w_uv'], (1, 0, 2)).astype(BF16),
        heads=heads, rank=rank)


def _key_width(rank):
    return -(-(rank + QK_ROPE) // 128) * 128


def _mla_pre(x, cos, sin, mw):
    n, d = x.shape
    heads, rank = mw['heads'], mw['rank']
    kw = _key_width(rank)
    tm = min(ROW_TILE, n)
    assert n % tm == 0 and cos.shape[0] % tm == 0
    tab_blocks = cos.shape[0] // tm
    row = lambda width: pl.BlockSpec((tm, width), lambda i: (i, 0))
    tab = pl.BlockSpec((tm, heads * QK_ROPE), lambda i: (i % tab_blocks, 0))
    names = ('kvn', 'wdkv', 'latn', 'wkr', 'wkrp', 'nmix', 'wdq', 'qn', 'wuqn', 'wuqr', 'wuqrp', 'wukt')
    consts = [mw[k] for k in names]
    scale = float((QK_NOPE + QK_ROPE) ** -0.5)
    return pl.pallas_call(
        functools.partial(_mla_pre_kernel, heads, scale),
        grid=(n // tm,),
        in_specs=[row(d), tab, tab] + [_const_spec(a.shape) for a in consts],
        out_specs=[row(rank), row(QK_ROPE), row(kw), row(heads * kw)],
        out_shape=[jax.ShapeDtypeStruct((n, rank), F32), jax.ShapeDtypeStruct((n, QK_ROPE), F32),
                   jax.ShapeDtypeStruct((n, kw), BF16), jax.ShapeDtypeStruct((n, heads * kw), BF16)],
        compiler_params=_params(("parallel",)),
    )(x, cos, sin, *consts)


def _softmax_update(s, vals, m_scr, l_scr, acc_scr):
    lanes = m_scr.shape[1]
    width = s.shape[1]
    m_prev = m_scr[...]
    m_new = jnp.maximum(m_prev, jnp.max(s, axis=-1, keepdims=True))
    alpha = jnp.exp(m_prev - m_new)
    m_wide = m_new[:, :width] if width <= lanes else jnp.tile(m_new, (1, width // lanes))
    p = jnp.exp(s - m_wide)
    l_scr[...] = alpha * l_scr[...] + jnp.sum(p, axis=-1, keepdims=True)
    acc_scr[...] = (jnp.tile(alpha, (1, acc_scr.shape[1] // lanes)) * acc_scr[...]
                    + jnp.dot(p.astype(BF16), vals, preferred_element_type=F32))
    m_scr[...] = m_new


def _softmax_init(m_scr, l_scr, acc_scr):
    m_scr[...] = jnp.full(m_scr.shape, MASK_VALUE, F32)
    l_scr[...] = jnp.zeros(l_scr.shape, F32)
    acc_scr[...] = jnp.zeros(acc_scr.shape, F32)


def _softmax_result(l_scr, acc_scr):
    return acc_scr[...] / jnp.tile(l_scr[...], (1, acc_scr.shape[1] // l_scr.shape[1]))


def _attn_prompt_kernel(heads, rank, tk, qc_ref, kc_ref, pkc_ref, o_ref, m_scr, l_scr, acc_scr):
    i = pl.program_id(1)
    tq = qc_ref.shape[0]
    kw = kc_ref.shape[2]
    group = heads // ATTN_ROW_PARTS
    rows = group * tq
    qcs = [jnp.concatenate([qc_ref[:, h * kw:(h + 1) * kw] for h in range(g * group, (g + 1) * group)], axis=0)
           for g in range(ATTN_ROW_PARTS)]
    stats = [(m_scr.at[pl.ds(g * rows, rows)], l_scr.at[pl.ds(g * rows, rows)], acc_scr.at[pl.ds(g * rows, rows)])
             for g in range(ATTN_ROW_PARTS)]
    _softmax_init(m_scr, l_scr, acc_scr)
    pkc = pkc_ref[...]
    for qc, st in zip(qcs, stats):
        _softmax_update(_mm_nt(qc, pkc), pkc[:, :rank], *st)
    n_full = (i * tq) // tk

    def body(jb, carry):
        kc = kc_ref[0, pl.ds(pl.multiple_of(jb * tk, tk), tk), :]
        for qc, st in zip(qcs, stats):
            _softmax_update(_mm_nt(qc, kc), kc[:, :rank], *st)
        return carry

    lax.fori_loop(0, n_full, body, 0)
    td = min(ATTN_DIAG_TILE, tk)
    first = n_full * tk
    q_pos = i * tq + lax.broadcasted_iota(jnp.int32, (rows, td), 0) % tq
    k_off = lax.broadcasted_iota(jnp.int32, (rows, td), 1)

    def diag_body(jb, carry):
        start = pl.multiple_of(first + jb * td, td)
        kc = kc_ref[0, pl.ds(start, td), :]
        for qc, st in zip(qcs, stats):
            _softmax_update(jnp.where(k_off + start <= q_pos, _mm_nt(qc, kc), MASK_VALUE), kc[:, :rank], *st)
        return carry

    lax.fori_loop(0, ((i + 1) * tq - first + td - 1) // td, diag_body, 0)
    o = _softmax_result(l_scr, acc_scr).astype(BF16)
    for h in range(heads):
        o_ref[:, h * rank:(h + 1) * rank] = o[h * tq:(h + 1) * tq, :]


def _attn_prompt(qc, kc, pkc, batch, heads, rank):
    n = qc.shape[0]
    t = n // batch
    kw = kc.shape[1]
    tq = min(ATTN_TILE, t)
    tk = min(ATTN_KEY_TILE, t)
    assert t % tk == 0 and tk % tq == 0 and tk % min(ATTN_DIAG_TILE, tk) == 0
    nq = t // tq
    return pl.pallas_call(
        functools.partial(_attn_prompt_kernel, heads, rank, tk),
        grid=(batch, nq),
        in_specs=[pl.BlockSpec((tq, heads * kw), lambda b, i: (b * nq + i, 0)),
                  pl.BlockSpec((1, t, kw), lambda b, i: (b, 0, 0)),
                  _const_spec(pkc.shape)],
        out_specs=pl.BlockSpec((tq, heads * rank), lambda b, i: (b * nq + i, 0)),
        out_shape=jax.ShapeDtypeStruct((n, heads * rank), BF16),
        scratch_shapes=[pltpu.VMEM((heads * tq, 128), F32), pltpu.VMEM((heads * tq, 128), F32),
                        pltpu.VMEM((heads * tq, rank), F32)],
        compiler_params=_params(("parallel", "arbitrary")),
    )(qc, kc.reshape(batch, t, kw), pkc)


def _attn_sample_kernel(chunk_pages, seq, rank, pt_ref, qc_ref, nkc_ref, lat_hbm, krt_hbm, o_ref,
                        lat_buf, krt_buf, sems, m_scr, l_scr, acc_scr):
    b = pl.program_id(0)
    qc = qc_ref[0]
    ahead = SAMPLE_SLOTS - 1
    total_chunks = pl.num_programs(0) * SAMPLE_CHUNKS

    def page_copies(number, p):
        slot = lax.rem(number, SAMPLE_SLOTS)
        page = pt_ref[number // SAMPLE_CHUNKS, lax.rem(number, SAMPLE_CHUNKS) * chunk_pages + p]
        rows = pl.ds(pl.multiple_of(p * PAGE_SIZE, PAGE_SIZE), PAGE_SIZE)
        return (pltpu.make_async_copy(lat_hbm.at[page], lat_buf.at[slot, rows, :], sems.at[slot, 0]),
                pltpu.make_async_copy(krt_hbm.at[page], krt_buf.at[slot, :, rows], sems.at[slot, 1]))

    def start_chunk(number):
        def body(p, carry):
            for cp in page_copies(number, p):
                cp.start()
            return carry
        lax.fori_loop(0, chunk_pages, body, 0, unroll=min(8, chunk_pages))

    def wait_chunk(number):
        slot = lax.rem(number, SAMPLE_SLOTS)
        pltpu.make_async_copy(lat_buf.at[slot], lat_buf.at[slot], sems.at[slot, 0]).wait()
        pltpu.make_async_copy(krt_buf.at[slot], krt_buf.at[slot], sems.at[slot, 1]).wait()

    @pl.when(b == 0)
    def _():
        for number in range(ahead):
            start_chunk(jnp.int32(number))

    _softmax_init(m_scr, l_scr, acc_scr)
    for chunk in range(SAMPLE_CHUNKS):
        number = b * SAMPLE_CHUNKS + chunk
        wait_chunk(number)

        @pl.when(number + ahead < total_chunks)
        def _():
            start_chunk(number + ahead)

        slot = lax.rem(number, SAMPLE_SLOTS)
        kl = lat_buf[slot].astype(BF16)
        krt = krt_buf[slot].astype(BF16)
        s = _mm_nt(qc[:, :rank], kl) + jnp.dot(qc[:, rank:rank + QK_ROPE], krt, preferred_element_type=F32)
        _softmax_update(s, kl, m_scr, l_scr, acc_scr)

    nkc = nkc_ref[0]
    s = _mm_nt(qc, nkc)
    q_t = lax.broadcasted_iota(jnp.int32, s.shape, 0) % seq
    k_t = lax.broadcasted_iota(jnp.int32, s.shape, 1)
    s = jnp.where(k_t <= q_t, s, MASK_VALUE)
    _softmax_update(s, nkc[:, :rank], m_scr, l_scr, acc_scr)
    o_ref[0] = _softmax_result(l_scr, acc_scr).astype(BF16)


def _attn_sample(qc, nkc, seq, rank, cache_latent, cache_krope_t, page_table):
    b, rows, kw = qc.shape
    total_pages = page_table.shape[1]
    assert total_pages % SAMPLE_CHUNKS == 0 and 2 <= SAMPLE_SLOTS <= b * SAMPLE_CHUNKS
    chunk_pages = total_pages // SAMPLE_CHUNKS
    chunk_keys = chunk_pages * PAGE_SIZE
    per_b = lambda arr: pl.BlockSpec((1,) + arr.shape[1:], lambda bi, pt: (bi, 0, 0))
    grid_spec = pltpu.PrefetchScalarGridSpec(
        num_scalar_prefetch=1,
        grid=(b,),
        in_specs=[per_b(qc), per_b(nkc), pl.BlockSpec(memory_space=pl.ANY), pl.BlockSpec(memory_space=pl.ANY)],
        out_specs=pl.BlockSpec((1, rows, rank), lambda bi, pt: (bi, 0, 0)),
        scratch_shapes=[pltpu.VMEM((SAMPLE_SLOTS, chunk_keys, rank), F32),
                        pltpu.VMEM((SAMPLE_SLOTS, QK_ROPE, chunk_keys), F32),
                        pltpu.SemaphoreType.DMA((SAMPLE_SLOTS, 2)),
                        pltpu.VMEM((rows, 128), F32), pltpu.VMEM((rows, 128), F32), pltpu.VMEM((rows, rank), F32)],
    )
    return pl.pallas_call(
        functools.partial(_attn_sample_kernel, chunk_pages, seq, rank),
        grid_spec=grid_spec,
        out_shape=jax.ShapeDtypeStruct((b, rows, rank), BF16),
        compiler_params=_params(("arbitrary",)),
    )(page_table, qc, nkc, cache_latent, cache_krope_t)


def _layer0(x, shift0, wkv0, p, flat):
    b, t, d = x.shape
    weights = _rwkv_pre_weights(p, 0)
    pre = _rwkv_pre_flat if flat else _rwkv_pre_seq
    seven, shift_new = pre(x, shift0, p['norm_mix'][0], weights)
    scan = _wkv_lanes if flat else _wkv_scan
    og, wkv_new = scan(seven, p['rw_rk'][0], p['rw_lnx_g'][0], p['rw_lnx_b'][0], wkv0)
    x1 = _post(og.reshape(b * t, d), x.reshape(b * t, d), p['rw_wo'][0].astype(BF16), p['norm_ffn'][0],
               p['ffn_up'][0].astype(BF16), p['ffn_down'][0].astype(BF16))
    return x1, wkv_new, shift_new


def _layer1_post(olat, x1, p, mw):
    return _post(olat, x1, p['w_o_mla'][0].astype(BF16), p['norm_ffn'][1], p['ffn_up'][1].astype(BF16),
                 p['ffn_down'][1].astype(BF16), wuv=mw['wuv'], fn=p['norm_final'])


def kernel(x_prompt, x_sample, state_wkv, state_shift, cache_latent, cache_krope, page_table,
           meta_tokens, rw_mu, rw_wr, rw_wk, rw_wv, rw_wo, rw_w0, rw_w1, rw_w2, rw_a0, rw_a1, rw_a2,
           rw_g1, rw_g2, rw_kk, rw_ka, rw_rk, rw_lnx_g, rw_lnx_b, norm_mix, norm_ffn, ffn_up, ffn_down,
           kv_norm, w_dkv, lat_norm, w_kr, w_uk, w_uv, w_dq, q_norm, w_uq, w_o_mla, norm_final):
    p = dict(rw_mu=rw_mu, rw_wr=rw_wr, rw_wk=rw_wk, rw_wv=rw_wv, rw_wo=rw_wo, rw_w0=rw_w0, rw_w1=rw_w1,
             rw_w2=rw_w2, rw_a0=rw_a0, rw_a1=rw_a1, rw_a2=rw_a2, rw_g1=rw_g1, rw_g2=rw_g2, rw_kk=rw_kk,
             rw_ka=rw_ka, rw_rk=rw_rk, rw_lnx_g=rw_lnx_g, rw_lnx_b=rw_lnx_b, norm_mix=norm_mix,
             norm_ffn=norm_ffn, ffn_up=ffn_up, ffn_down=ffn_down, kv_norm=kv_norm, w_dkv=w_dkv,
             lat_norm=lat_norm, w_kr=w_kr, w_uk=w_uk, w_uv=w_uv, w_dq=w_dq, q_norm=q_norm, w_uq=w_uq,
             w_o_mla=w_o_mla, norm_final=norm_final)
    assert rw_wr.shape[0] == 1 and w_dq.shape[0] == 1, "one RWKV layer followed by one MLA layer"
    bsz, seq, d = x_prompt.shape
    dbsz, dseq, _ = x_sample.shape
    rw_heads = d // RW_HEAD
    mw = _mla_weights(p, 0)
    heads, rank = mw['heads'], mw['rank']
    n_meta = meta_tokens.shape[0]
    past_len = page_table.shape[1] * PAGE_SIZE

    xm1, wkv_m, shift_m = _layer0(meta_tokens[None].astype(F32), jnp.zeros((1, d), F32),
                                  jnp.zeros((1, rw_heads, RW_HEAD, RW_HEAD), F32), p, flat=False)
    cos_m, sin_m = _rope_tables(jnp.arange(n_meta, dtype=jnp.int32), heads)
    lat_m, kr_m, kc_m, _ = _mla_pre(xm1, cos_m, sin_m, mw)

    xp1, wkv_p, shift_p = _layer0(x_prompt, jnp.broadcast_to(shift_m, (bsz, d)),
                                  jnp.broadcast_to(wkv_m, (bsz,) + wkv_m.shape[1:]), p, flat=False)
    cos_p, sin_p = _rope_tables(n_meta + jnp.arange(seq, dtype=jnp.int32), heads)
    lat_p, kr_p, kc_p, qc_p = _mla_pre(xp1, cos_p, sin_p, mw)
    olat_p = _attn_prompt(qc_p, kc_p, kc_m, bsz, heads, rank)
    y_prompt = _layer1_post(olat_p, xp1, p, mw).reshape(bsz, seq, d)
    latent_prompt = jnp.concatenate(
        [jnp.broadcast_to(lat_m[None], (bsz, n_meta, rank)), lat_p.reshape(bsz, seq, rank)], axis=1)
    krope_prompt = jnp.concatenate(
        [jnp.broadcast_to(kr_m[None], (bsz, n_meta, QK_ROPE)), kr_p.reshape(bsz, seq, QK_ROPE)], axis=1)

    xs1, wkv_s, shift_s = _layer0(x_sample, state_shift[0], jnp.transpose(state_wkv[0], (1, 2, 3, 0)), p, flat=True)
    wkv_s = jnp.transpose(wkv_s, (3, 0, 1, 2))
    cos_s, sin_s = _rope_tables(past_len + jnp.arange(dseq, dtype=jnp.int32), heads)
    cos_s, sin_s = jnp.tile(cos_s, (dbsz, 1)), jnp.tile(sin_s, (dbsz, 1))
    lat_s, kr_s, kc_s, qc_s = _mla_pre(xs1, cos_s, sin_s, mw)
    kw = kc_s.shape[1]
    qc_s = jnp.transpose(qc_s.reshape(dbsz, dseq, heads, kw), (0, 2, 1, 3)).reshape(dbsz, heads * dseq, kw)
    new_rows = -(-dseq // 16) * 16
    nkc_s = jnp.pad(kc_s.reshape(dbsz, dseq, kw), ((0, 0), (0, new_rows - dseq), (0, 0)))
    olat_s = _attn_sample(qc_s, nkc_s, dseq, rank, cache_latent, jnp.swapaxes(cache_krope, 1, 2), page_table)
    olat_s = jnp.transpose(olat_s.reshape(dbsz, heads, dseq, rank), (0, 2, 1, 3)).reshape(dbsz * dseq, heads * rank)
    y_sample = _layer1_post(olat_s, xs1, p, mw).reshape(dbsz, dseq, d)

    return (y_prompt, y_sample, wkv_p[None], shift_p[None], latent_prompt, krope_prompt,
            wkv_s[None], shift_s[None], lat_s.reshape(dbsz, dseq, rank), kr_s.reshape(dbsz, dseq, QK_ROPE))
```
